```python
import math
import jax, jax.numpy as jnp
from jax import lax
import numpy as np

D_MODEL = 2048
BATCH = 1
SEQ = 16384
DEPTH = 4

N_MIXERS = 2
N_MLA_LAYERS = (DEPTH + 1) // 2
N_SSM_LAYERS = DEPTH // 2
D_FF = 4 * D_MODEL
RMS_EPS = 1e-6

MLA_HEADS = 16
QK_NOPE_DIM = 128
QK_ROPE_DIM = 64
QK_HEAD_DIM = QK_NOPE_DIM + QK_ROPE_DIM
V_HEAD_DIM = 128
Q_LORA_RANK = 512
KV_LORA_RANK = 512
MLA_IN_DIM = Q_LORA_RANK + KV_LORA_RANK + QK_ROPE_DIM
ROPE_THETA = 10000.0
Q_BLOCK = 128

SSM_EXPAND = 2
SSM_D_INNER = SSM_EXPAND * D_MODEL
SSM_HEAD_DIM = 64
SSM_HEADS = SSM_D_INNER // SSM_HEAD_DIM
SSM_GROUPS = 8
SSM_HEADS_PER_GROUP = SSM_HEADS // SSM_GROUPS
SSM_STATE = 128
SSM_CONV_WIDTH = 4
SSM_CHUNK = 256
SSM_CONV_DIM = SSM_D_INNER + 2 * SSM_GROUPS * SSM_STATE
SSM_IN_DIM = SSM_D_INNER + SSM_CONV_DIM + SSM_HEADS

kernel_name = "hybrid_mla_mamba2_sqrelu_trunk"


def rms_norm(x, g):
    xf = x.astype(jnp.float32)
    y = xf * lax.rsqrt(jnp.mean(xf * xf, axis=-1, keepdims=True) + RMS_EPS)
    return (y * g.astype(jnp.float32)).astype(x.dtype)


def rope_tables(positions):
    inv_freq = ROPE_THETA ** (-jnp.arange(0, QK_ROPE_DIM, 2, dtype=jnp.float32) / QK_ROPE_DIM)
    ang = positions.astype(jnp.float32)[..., None] * inv_freq
    return jnp.cos(ang), jnp.sin(ang)


def apply_rope(x, cos, sin):
    x1, x2 = jnp.split(x, 2, axis=-1)
    c = cos[:, :, None, :].astype(x.dtype)
    s = sin[:, :, None, :].astype(x.dtype)
    return jnp.concatenate([x1 * c - x2 * s, x2 * c + x1 * s], axis=-1)


def causal_block_attention(q, k, v):
    B, S, H, Dq = q.shape
    n_blk = S // Q_BLOCK
    scale = Dq ** -0.5
    q_blocks = q.reshape(B, n_blk, Q_BLOCK, H, Dq).transpose(1, 0, 2, 3, 4)
    k_pos = jnp.arange(S)

    def one_block(args):
        blk_idx, q_blk = args
        s = jnp.einsum('bqhd,bkhd->bhqk', q_blk, k, preferred_element_type=jnp.float32) * scale
        q_pos = blk_idx * Q_BLOCK + jnp.arange(Q_BLOCK)
        mask = k_pos[None, :] <= q_pos[:, None]
        p = jax.nn.softmax(jnp.where(mask, s, -jnp.inf), axis=-1).astype(v.dtype)
        return jnp.einsum('bhqk,bkhd->bqhd', p, v)

    o = lax.map(one_block, (jnp.arange(n_blk), q_blocks))
    return o.transpose(1, 0, 2, 3, 4).reshape(B, S, H, v.shape[-1])


def mla_mixer(h, cos, sin, w_in, q_norm_g, w_uq, kv_norm_g, w_ukv, qn_g, kn_g, w_o):
    B, S, _ = h.shape
    a = h @ w_in
    c_q, c_kv, k_rope = jnp.split(a, [Q_LORA_RANK, Q_LORA_RANK + KV_LORA_RANK], axis=-1)
    q = (rms_norm(c_q, q_norm_g) @ w_uq).reshape(B, S, MLA_HEADS, QK_HEAD_DIM)
    kv = (rms_norm(c_kv, kv_norm_g) @ w_ukv).reshape(B, S, MLA_HEADS, QK_NOPE_DIM + V_HEAD_DIM)
    k_nope, v = jnp.split(kv, [QK_NOPE_DIM], axis=-1)
    k_rope = jnp.broadcast_to(k_rope[:, :, None, :], (B, S, MLA_HEADS, QK_ROPE_DIM))
    k = jnp.concatenate([k_nope, k_rope], axis=-1)
    q = rms_norm(q, qn_g)
    k = rms_norm(k, kn_g)
    q = jnp.concatenate([q[..., :QK_NOPE_DIM], apply_rope(q[..., QK_NOPE_DIM:], cos, sin)], axis=-1)
    k = jnp.concatenate([k[..., :QK_NOPE_DIM], apply_rope(k[..., QK_NOPE_DIM:], cos, sin)], axis=-1)
    o = causal_block_attention(q, k, v)
    return o.reshape(B, S, MLA_HEADS * V_HEAD_DIM) @ w_o


def causal_depthwise_conv(x, w, b):
    C = x.shape[-1]
    out = lax.conv_general_dilated(
        x, w[:, None, :], window_strides=(1,), padding=[(SSM_CONV_WIDTH - 1, 0)],
        dimension_numbers=('NWC', 'WIO', 'NWC'), feature_group_count=C)
    return out + b


def ssd_chunked_scan(x, dt, a, b, c):
    B, S, G, HG, P = x.shape
    N = b.shape[-1]
    L = math.gcd(S, SSM_CHUNK)
    nc = S // L
    xdt = x.astype(jnp.float32) * dt[..., None]
    log_a = dt * a

    def to_chunks(t):
        return t.reshape((B, nc, L) + t.shape[2:]).swapaxes(0, 1)

    xs = (to_chunks(xdt), to_chunks(log_a),
          to_chunks(b.astype(jnp.float32)), to_chunks(c.astype(jnp.float32)))
    tril = jnp.tril(jnp.ones((L, L), dtype=bool))

    def chunk_step(state, inp):
        x_c, la_c, b_c, c_c = inp
        cum = jnp.cumsum(la_c, axis=1)
        diff = cum[:, :, None] - cum[:, None, :]
        decay = jnp.exp(jnp.where(tril[None, :, :, None, None], diff, -jnp.inf))
        cb = jnp.einsum('blgn,bsgn->blsg', c_c, b_c)
        y_diag = jnp.einsum('blsgh,bsghp->blghp', cb[..., None] * decay, x_c)
        y_off = jnp.einsum('blgn,bghpn->blghp', c_c, state) * jnp.exp(cum)[..., None]
        decay_to_end = jnp.exp(cum[:, -1:] - cum)
        new_state = state * jnp.exp(cum[:, -1])[..., None, None] + jnp.einsum(
            'blgn,blghp->bghpn', b_c, x_c * decay_to_end[..., None])
        return new_state, y_diag + y_off

    state0 = jnp.zeros((B, G, HG, P, N), jnp.float32)
    _, ys = lax.scan(chunk_step, state0, xs)
    return ys.swapaxes(0, 1).reshape(B, S, G, HG, P)


def ssm_mixer(h, w_in, conv_w, conv_b, dt_bias, a_log, d_skip, norm_g, w_out):
    B, S, _ = h.shape
    zxbcdt = h @ w_in
    z, xbc, dt = jnp.split(zxbcdt, [SSM_D_INNER, SSM_D_INNER + SSM_CONV_DIM], axis=-1)
    xbc = jax.nn.silu(causal_depthwise_conv(xbc, conv_w, conv_b))
    xs, b_in, c_in = jnp.split(xbc, [SSM_D_INNER, SSM_D_INNER + SSM_GROUPS * SSM_STATE], axis=-1)
    xs = xs.reshape(B, S, SSM_GROUPS, SSM_HEADS_PER_GROUP, SSM_HEAD_DIM)
    b_in = b_in.reshape(B, S, SSM_GROUPS, SSM_STATE)
    c_in = c_in.reshape(B, S, SSM_GROUPS, SSM_STATE)
    dt = jax.nn.softplus(dt.astype(jnp.float32) + dt_bias.astype(jnp.float32))
    dt = dt.reshape(B, S, SSM_GROUPS, SSM_HEADS_PER_GROUP)
    a = -jnp.exp(a_log.astype(jnp.float32)).reshape(SSM_GROUPS, SSM_HEADS_PER_GROUP)
    y = ssd_chunked_scan(xs, dt, a, b_in, c_in)
    y = y + d_skip.astype(jnp.float32).reshape(SSM_GROUPS, SSM_HEADS_PER_GROUP)[..., None] * xs.astype(jnp.float32)
    y = y.reshape(B, S, SSM_D_INNER) * jax.nn.silu(z.astype(jnp.float32))
    y = y.reshape(B, S, SSM_GROUPS, SSM_D_INNER // SSM_GROUPS)
    y = y * lax.rsqrt(jnp.mean(y * y, axis=-1, keepdims=True) + RMS_EPS)
    y = (y.reshape(B, S, SSM_D_INNER) * norm_g.astype(jnp.float32)).astype(h.dtype)
    return y @ w_out


def squared_relu_mlp(h, w_in, w_out):
    u = jax.nn.relu(h @ w_in)
    return (u * u) @ w_out


def setup_inputs(seed: int = 0) -> dict:
    key = jax.random.key(seed)
    ks = jax.random.split(key, 24)
    f32 = jnp.float32
    res_scale = (2 * DEPTH) ** -0.5

    def w(k, shape, fan_in, scale=1.0):
        return jax.random.normal(k, shape, f32) * (scale * fan_in ** -0.5)

    def gain(k, shape):
        return 1.0 + 0.02 * jax.random.normal(k, shape, f32)

    NA, NB = N_MLA_LAYERS, N_SSM_LAYERS
    x = jax.random.normal(ks[0], (BATCH, SEQ, D_MODEL), f32)
    offset = jax.random.randint(ks[1], (BATCH, 1), 0, 4096, dtype=jnp.int32)
    positions = offset + jnp.arange(SEQ, dtype=jnp.int32)[None, :]
    dt0 = jnp.exp(jax.random.uniform(ks[17], (NB, SSM_HEADS), f32,
                                     minval=math.log(1e-3), maxval=math.log(1e-1)))
    return {
        "x": x,
        "positions": positions,
        "mix_norm_g": gain(ks[2], (DEPTH, D_MODEL)),
        "mlp_norm_g": gain(ks[3], (DEPTH, D_MODEL)),
        "mlp_w_in": w(ks[4], (DEPTH, D_MODEL, D_FF), D_MODEL),
        "mlp_w_out": w(ks[5], (DEPTH, D_FF, D_MODEL), D_FF, res_scale),
        "mla_w_in": w(ks[6], (NA, D_MODEL, MLA_IN_DIM), D_MODEL),
        "mla_q_norm_g": gain(ks[7], (NA, Q_LORA_RANK)),
        "mla_w_uq": w(ks[8], (NA, Q_LORA_RANK, MLA_HEADS * QK_HEAD_DIM), Q_LORA_RANK),
        "mla_kv_norm_g": gain(ks[9], (NA, KV_LORA_RANK)),
        "mla_w_ukv": w(ks[10], (NA, KV_LORA_RANK, MLA_HEADS * (QK_NOPE_DIM + V_HEAD_DIM)), KV_LORA_RANK),
        "mla_qk_norm_q": gain(ks[11], (NA, QK_HEAD_DIM)),
        "mla_qk_norm_k": gain(ks[12], (NA, QK_HEAD_DIM)),
        "mla_w_o": w(ks[13], (NA, MLA_HEADS * V_HEAD_DIM, D_MODEL), MLA_HEADS * V_HEAD_DIM, res_scale),
        "ssm_w_in": w(ks[14], (NB, D_MODEL, SSM_IN_DIM), D_MODEL),
        "ssm_conv_w": w(ks[15], (NB, SSM_CONV_WIDTH, SSM_CONV_DIM), SSM_CONV_WIDTH),
        "ssm_conv_b": 0.02 * jax.random.normal(ks[16], (NB, SSM_CONV_DIM), f32),
        "ssm_dt_bias": dt0 + jnp.log(-jnp.expm1(-dt0)),
        "ssm_a_log": jnp.log(jax.random.uniform(ks[18], (NB, SSM_HEADS), f32, minval=1.0, maxval=16.0)),
        "ssm_d": gain(ks[19], (NB, SSM_HEADS)),
        "ssm_norm_g": gain(ks[20], (NB, SSM_D_INNER)),
        "ssm_w_out": w(ks[21], (NB, SSM_D_INNER, D_MODEL), SSM_D_INNER, res_scale),
    }


def reference(x, positions, mix_norm_g, mlp_norm_g, mlp_w_in, mlp_w_out,
              mla_w_in, mla_q_norm_g, mla_w_uq, mla_kv_norm_g, mla_w_ukv,
              mla_qk_norm_q, mla_qk_norm_k, mla_w_o,
              ssm_w_in, ssm_conv_w, ssm_conv_b, ssm_dt_bias, ssm_a_log, ssm_d,
              ssm_norm_g, ssm_w_out):
    cos, sin = rope_tables(positions)
    for i in range(DEPTH):
        h = rms_norm(x, mix_norm_g[i])
        j = i // N_MIXERS
        if i % N_MIXERS == 0:
            mix = mla_mixer(h, cos, sin, mla_w_in[j], mla_q_norm_g[j], mla_w_uq[j],
                            mla_kv_norm_g[j], mla_w_ukv[j], mla_qk_norm_q[j],
                            mla_qk_norm_k[j], mla_w_o[j])
        else:
            mix = ssm_mixer(h, ssm_w_in[j], ssm_conv_w[j], ssm_conv_b[j], ssm_dt_bias[j],
                            ssm_a_log[j], ssm_d[j], ssm_norm_g[j], ssm_w_out[j])
        x = x + mix
        h = rms_norm(x, mlp_norm_g[i])
        x = x + squared_relu_mlp(h, mlp_w_in[i], mlp_w_out[i])
    return x
```

```python
import functools

import jax
import jax.numpy as jnp
from jax import lax
from jax.experimental import pallas as pl
from jax.experimental.pallas import tpu as pltpu

F32 = jnp.float32
BF16 = jnp.bfloat16

RMS_EPS = 1e-6

MLA_HEADS = 16
QK_NOPE_DIM = 128
QK_ROPE_DIM = 64
QK_HEAD_DIM = QK_NOPE_DIM + QK_ROPE_DIM
V_HEAD_DIM = 128
Q_LORA_RANK = 512
KV_LORA_RANK = 512
ROPE_THETA = 10000.0
SSM_HEAD_DIM = 64
SSM_GROUPS = 8
SSM_STATE = 128
SSM_CONV_WIDTH = 4
SSM_CHUNK = 256

LANES = 128
SUBLANES = 8
VMEM_LIMIT_BYTES = 56 * 1024 * 1024

QK_PAD_DIM = 2 * LANES
HALF_ROPE = QK_ROPE_DIM // 2

MASK_VALUE = -1e30


def _tiles(S):
    return dict(
        mlp_tm=min(512, S), mlp_tf=512,
        proj_tm=min(256, S),
        attn_bq=min(1024, S), attn_bk=min(512, S),
        ssm_tm=min(512, S), ssm_tn=1024,
        out_tm=min(512, S),
    )


def _params(*sem):
    return pltpu.CompilerParams(dimension_semantics=sem, vmem_limit_bytes=VMEM_LIMIT_BYTES)


def _rms_rows(x, g):
    ms = jnp.mean(x * x, axis=-1, keepdims=True)
    return x * lax.rsqrt(ms + RMS_EPS) * g


def _resident(shape):
    nd = len(shape)
    return pl.BlockSpec(shape, lambda *_: (0,) * nd, pipeline_mode=pl.Buffered(1))


def _mlp_body(x_ref, g_ref, w1_ref, w2_ref, o_ref, h_ref):
    @pl.when(pl.program_id(1) == 0)
    def _():
        x = x_ref[...]
        h_ref[...] = _rms_rows(x, g_ref[...]).astype(BF16)
        o_ref[...] = x

    u = jnp.dot(h_ref[...], w1_ref[...], preferred_element_type=F32)
    u = jnp.maximum(u, 0.0)
    u = (u * u).astype(BF16)
    o_ref[...] += jnp.dot(u, w2_ref[...], preferred_element_type=F32)


def _mlp(x, g, w1, w2, tm, tf):
    S, D = x.shape
    F = w1.shape[1]
    return pl.pallas_call(
        _mlp_body,
        grid=(S // tm, F // tf),
        in_specs=[
            pl.BlockSpec((tm, D), lambda i, f: (i, 0)),
            pl.BlockSpec((1, D), lambda i, f: (0, 0)),
            pl.BlockSpec((D, tf), lambda i, f: (0, f)),
            pl.BlockSpec((tf, D), lambda i, f: (f, 0)),
        ],
        out_specs=pl.BlockSpec((tm, D), lambda i, f: (i, 0)),
        out_shape=jax.ShapeDtypeStruct((S, D), F32),
        scratch_shapes=[pltpu.VMEM((tm, D), BF16)],
        compiler_params=_params("parallel", "arbitrary"),
        name="mlp",
    )(x, g, w1, w2)


def _resmm_body(x_ref, a_ref, w_ref, o_ref):
    o_ref[...] = x_ref[...] + jnp.dot(a_ref[...], w_ref[...], preferred_element_type=F32)


def _residual_matmul(x, a, w, tm):
    S, D = x.shape
    K = a.shape[1]
    return pl.pallas_call(
        _resmm_body,
        grid=(S // tm,),
        in_specs=[
            pl.BlockSpec((tm, D), lambda i: (i, 0)),
            pl.BlockSpec((tm, K), lambda i: (i, 0)),
            _resident((K, D)),
        ],
        out_specs=pl.BlockSpec((tm, D), lambda i: (i, 0)),
        out_shape=jax.ShapeDtypeStruct((S, D), F32),
        compiler_params=_params("parallel"),
        name="residual_matmul",
    )(x, a, w)


def _mla_proj_body(x_ref, pos_ref, g_ref, win_ref, qlg_ref, wuq_ref, kvlg_ref, wukv_ref,
                   qgain_ref, kgain_ref, rope_ref, q_ref, k_ref, v_ref):
    h = _rms_rows(x_ref[...], g_ref[...]).astype(BF16)
    a = jnp.dot(h, win_ref[...], preferred_element_type=F32)
    cq = _rms_rows(a[:, :Q_LORA_RANK], qlg_ref[...]).astype(BF16)
    ckv = _rms_rows(a[:, Q_LORA_RANK:Q_LORA_RANK + KV_LORA_RANK], kvlg_ref[...]).astype(BF16)
    kr = a[:, Q_LORA_RANK + KV_LORA_RANK:]
    q = jnp.dot(cq, wuq_ref[...], preferred_element_type=F32)
    kv = jnp.dot(ckv, wukv_ref[...], preferred_element_type=F32)

    ang = pos_ref[...].astype(F32) * rope_ref[0:1, :]
    cos_t = jnp.cos(ang) * rope_ref[1:2, :]
    sin_t = jnp.sin(ang) * rope_ref[2:3, :]

    def rope(v):
        return v * cos_t + pltpu.roll(v, 2 * HALF_ROPE, 1) * sin_t

    qgain = qgain_ref[...]
    kgain = kgain_ref[...]
    inv_d = 1.0 / QK_HEAD_DIM
    kr_ss = 0.5 * jnp.sum(kr * kr, axis=-1, keepdims=True)
    kr_rot = rope(kr * kgain[:, LANES:])
    for hd in range(MLA_HEADS):
        lo = hd * QK_PAD_DIM
        mid = lo + LANES
        hi = lo + QK_PAD_DIM
        qn = q[:, lo:mid]
        qr = q[:, mid:hi]
        q_ss = jnp.sum(qn * qn, axis=-1, keepdims=True) + 0.5 * jnp.sum(qr * qr, axis=-1, keepdims=True)
        q_r = lax.rsqrt(q_ss * inv_d + RMS_EPS)
        q_ref[:, lo:mid] = (qn * q_r * qgain[:, :LANES]).astype(BF16)
        q_ref[:, mid:hi] = rope(qr * q_r * qgain[:, LANES:]).astype(BF16)
        kn = kv[:, lo:mid]
        k_ss = jnp.sum(kn * kn, axis=-1, keepdims=True) + kr_ss
        k_r = lax.rsqrt(k_ss * inv_d + RMS_EPS)
        k_ref[:, lo:mid] = (kn * k_r * kgain[:, :LANES]).astype(BF16)
        k_ref[:, mid:hi] = (kr_rot * k_r).astype(BF16)
        v_ref[:, hd * V_HEAD_DIM:(hd + 1) * V_HEAD_DIM] = kv[:, mid:hi].astype(BF16)


def _mla_proj(x, pos, g, win, qlg, wuq, kvlg, wukv, qgain, kgain, rope_tab, tm):
    S, D = x.shape
    HQ = MLA_HEADS * QK_PAD_DIM
    HV = MLA_HEADS * V_HEAD_DIM
    row = lambda w: pl.BlockSpec((tm, w), lambda i: (i, 0))
    return pl.pallas_call(
        _mla_proj_body,
        grid=(S // tm,),
        in_specs=[
            row(D), row(1), _resident(g.shape), _resident(win.shape), _resident(qlg.shape),
            _resident(wuq.shape), _resident(kvlg.shape), _resident(wukv.shape),
            _resident(qgain.shape), _resident(kgain.shape), _resident(rope_tab.shape),
        ],
        out_specs=[row(HQ), row(HQ), row(HV)],
        out_shape=[
            jax.ShapeDtypeStruct((S, HQ), BF16),
            jax.ShapeDtypeStruct((S, HQ), BF16),
            jax.ShapeDtypeStruct((S, HV), BF16),
        ],
        compiler_params=_params("parallel"),
        name="mla_proj",
    )(x, pos, g, win, qlg, wuq, kvlg, wukv, qgain, kgain, rope_tab)


def _attn_body(q_ref, k_ref, vt_ref, o_ref, m_ref, l_ref, acc_ref, *, bq, bk):
    qi = pl.program_id(1)
    q = q_ref[...]
    m_ref[...] = jnp.full(m_ref.shape, MASK_VALUE, F32)
    l_ref[...] = jnp.zeros(l_ref.shape, F32)
    acc_ref[...] = jnp.zeros(acc_ref.shape, F32)

    def step(kb, mask):
        kblk = k_ref[pl.ds(pl.multiple_of(kb * bk, bk), bk), :]
        s = lax.dot_general(kblk, q, (((1,), (1,)), ((), ())), preferred_element_type=F32)
        if mask is not None:
            s = jnp.where(mask, s, MASK_VALUE)
        m_old = m_ref[...]
        m_new = jnp.maximum(m_old, jnp.max(s, axis=0, keepdims=True))
        p = jnp.exp(s - m_new)
        alpha = jnp.exp(m_old - m_new)
        l_ref[...] = alpha * l_ref[...] + jnp.sum(p, axis=0, keepdims=True)
        pv = jnp.dot(vt_ref[kb], p.astype(BF16), preferred_element_type=F32)
        acc_ref[...] = alpha * acc_ref[...] + pv
        m_ref[...] = m_new

    per_q = bq // bk
    n_full = qi * per_q

    def full_step(kb, carry):
        step(kb, None)
        return carry

    lax.fori_loop(0, n_full, full_step, 0)
    key = lax.broadcasted_iota(jnp.int32, (bk, bq), 0)
    qry = lax.broadcasted_iota(jnp.int32, (bk, bq), 1)
    for d in range(per_q):
        step(n_full + d, key + d * bk <= qry)
    o_ref[...] = (acc_ref[...] / l_ref[...]).astype(o_ref.dtype)


def _attention(q, k, vt, bq, bk):
    S = q.shape[0]
    H = MLA_HEADS
    nkb = S // bk
    return pl.pallas_call(
        functools.partial(_attn_body, bq=bq, bk=bk),
        grid=(H, S // bq),
        in_specs=[
            pl.BlockSpec((bq, QK_PAD_DIM), lambda h, i: (i, h)),
            pl.BlockSpec((S, QK_PAD_DIM), lambda h, i: (0, h)),
            pl.BlockSpec((None, nkb, V_HEAD_DIM, bk), lambda h, i: (h, 0, 0, 0)),
        ],
        out_specs=pl.BlockSpec((V_HEAD_DIM, bq), lambda h, i: (h, i)),
        out_shape=jax.ShapeDtypeStruct((H * V_HEAD_DIM, S), BF16),
        scratch_shapes=[
            pltpu.VMEM((1, bq), F32),
            pltpu.VMEM((1, bq), F32),
            pltpu.VMEM((V_HEAD_DIM, bq), F32),
        ],
        compiler_params=_params("parallel", "arbitrary"),
        name="attention",
    )(q, k, vt)


def _silu(x):
    return x * jax.nn.sigmoid(x)


def _ssm_in_body(x_ref, g_ref, w_ref, wdt_ref, dtb_ref, cw_ref, cb_ref,
                 z_ref, xbc_ref, dt_ref, h_ref, buf_ref, carry_ref, *, nz, tm):
    i = pl.program_id(0)
    n = pl.program_id(1)

    @pl.when(n == 0)
    def _():
        h = _rms_rows(x_ref[...], g_ref[...]).astype(BF16)
        h_ref[...] = h
        dt_raw = jnp.dot(h, wdt_ref[...], preferred_element_type=F32)
        dt_ref[...] = jax.nn.softplus(dt_raw + dtb_ref[...])

    pre = jnp.dot(h_ref[...], w_ref[...], preferred_element_type=F32)

    @pl.when(n < nz)
    def _():
        z_ref[...] = _silu(pre)

    @pl.when(n >= nz)
    def _():
        j = n - nz

        @pl.when(i == 0)
        def _():
            carry_ref[j] = jnp.zeros(carry_ref.shape[1:], F32)

        buf_ref[0:SUBLANES, :] = carry_ref[j]
        buf_ref[SUBLANES:, :] = pre
        carry_ref[j] = pre[tm - SUBLANES:, :]
        cw = cw_ref[...]
        acc = cb_ref[...] + cw[SSM_CONV_WIDTH - 1:SSM_CONV_WIDTH, :] * pre
        for kk in range(SSM_CONV_WIDTH - 1):
            shift = SSM_CONV_WIDTH - 1 - kk
            acc = acc + cw[kk:kk + 1, :] * buf_ref[pl.ds(SUBLANES - shift, tm), :]
        xbc_ref[...] = _silu(acc)


def _ssm_in(x, g, w, wdt, dtb, cw, cb, d_inner, tm, tn):
    S, D = x.shape
    N = w.shape[1]
    nz = d_inner // tn
    nx = (N - d_inner) // tn
    zmap = lambda i, n: (i, jnp.minimum(n, nz - 1))
    xmap = lambda i, n: (i, jnp.maximum(n - nz, 0))
    cmap = lambda i, n: (0, jnp.maximum(n - nz, 0))
    return pl.pallas_call(
        functools.partial(_ssm_in_body, nz=nz, tm=tm),
        grid=(S // tm, nz + nx),
        in_specs=[
            pl.BlockSpec((tm, D), lambda i, n: (i, 0)),
            pl.BlockSpec((1, D), lambda i, n: (0, 0)),
            pl.BlockSpec((D, tn), lambda i, n: (0, n)),
            pl.BlockSpec((D, LANES), lambda i, n: (0, 0)),
            pl.BlockSpec((1, LANES), lambda i, n: (0, 0)),
            pl.BlockSpec((SSM_CONV_WIDTH, tn), cmap),
            pl.BlockSpec((1, tn), cmap),
        ],
        out_specs=[
            pl.BlockSpec((tm, tn), zmap),
            pl.BlockSpec((tm, tn), xmap),
            pl.BlockSpec((tm, LANES), lambda i, n: (i, 0)),
        ],
        out_shape=[
            jax.ShapeDtypeStruct((S, d_inner), F32),
            jax.ShapeDtypeStruct((S, N - d_inner), F32),
            jax.ShapeDtypeStruct((S, LANES), F32),
        ],
        scratch_shapes=[
            pltpu.VMEM((tm, D), BF16),
            pltpu.VMEM((tm + SUBLANES, tn), F32),
            pltpu.VMEM((nx, SUBLANES, tn), F32),
        ],
        compiler_params=_params("arbitrary", "arbitrary"),
        name="ssm_in",
    )(x, g, w, wdt, dtb, cw, cb)


def _split3(a):
    hi = a.astype(BF16)
    r1 = a - hi.astype(F32)
    mid = r1.astype(BF16)
    lo = (r1 - mid.astype(F32)).astype(BF16)
    return hi, mid, lo


def _ssd_body(xs_ref, b_ref, c_ref, z_ref, dtc_ref, dtr_ref, ac_ref, ar_ref, d_ref, ng_ref,
              tl_ref, tu_ref, y_ref, st_ref, *, L, hpg):
    c = pl.program_id(0)
    g = pl.program_id(1)

    @pl.when(c == 0)
    def _():
        st_ref[g] = jnp.zeros(st_ref.shape[1:], F32)

    xs = xs_ref[...]
    bmat = b_ref[...]
    cmat = c_ref[...].astype(BF16)
    dt = dtc_ref[...]
    la = dt * ac_ref[...]
    la_t = dtr_ref[...] * ar_ref[:, 0:1]
    cum = sum(jnp.dot(tl_ref[...], p, preferred_element_type=F32) for p in _split3(la))
    cum_t = sum(jnp.dot(p, tu_ref[...], preferred_element_type=F32) for p in _split3(la_t))

    cb = lax.dot_general(cmat, bmat.astype(BF16), (((1,), (1,)), ((), ())),
                         preferred_element_type=F32)
    t_out = lax.broadcasted_iota(jnp.int32, (L, L), 0)
    t_in = lax.broadcasted_iota(jnp.int32, (L, L), 1)
    causal = t_in <= t_out
    first_head = lax.broadcasted_iota(jnp.int32, (L, LANES), 1) < SSM_HEAD_DIM

    state = st_ref[g]
    y_off = jnp.dot(cmat, state.astype(BF16), preferred_element_type=F32)
    cum_last = cum[L - 1:L, :]

    def col(a, j, rows):
        return jnp.broadcast_to(a[:, j:j + 1], (rows, LANES))

    ys, xws, cds = [], [], []
    for pr in range(hpg // 2):
        j0, j1 = 2 * pr, 2 * pr + 1
        sl = slice(pr * LANES, (pr + 1) * LANES)
        cum_p = jnp.where(first_head, col(cum, j0, L), col(cum, j1, L))
        dt_p = jnp.where(first_head, col(dt, j0, L), col(dt, j1, L))
        last_p = jnp.where(first_head[:1], col(cum_last, j0, 1), col(cum_last, j1, 1))
        xdt = xs[:, sl] * dt_p
        ms = []
        for j in (j0, j1):
            diff = jnp.broadcast_to(cum[:, j:j + 1], (L, L)) - cum_t[j:j + 1, :]
            decay = jnp.exp(jnp.where(causal, diff, -jnp.inf))
            ms.append((cb * decay).astype(BF16))
        lhs = jnp.concatenate(ms, axis=1)
        rhs = jnp.concatenate([jnp.where(first_head, xdt, 0.0).astype(BF16),
                               jnp.where(first_head, 0.0, xdt).astype(BF16)], axis=0)
        y_diag = jnp.dot(lhs, rhs, preferred_element_type=F32)
        ys.append(y_diag + y_off[:, sl] * jnp.exp(cum_p))
        xws.append((xdt * jnp.exp(last_p - cum_p)).astype(BF16))
        cds.append(jnp.exp(last_p))

    xw = jnp.concatenate(xws, axis=1)
    chunk_decay = jnp.concatenate(cds, axis=1)
    st_ref[g] = state * chunk_decay + jnp.dot(bmat.T.astype(BF16), xw, preferred_element_type=F32)

    y = jnp.concatenate(ys, axis=1) + d_ref[...] * xs
    y = y * z_ref[...]
    y = y * lax.rsqrt(jnp.mean(y * y, axis=-1, keepdims=True) + RMS_EPS)
    y_ref[...] = (y * ng_ref[...]).astype(BF16)


def _ssd(xbc, zs, dtc, dtr, a_col, a_row, d_tab, ng, tri_l, tri_u, d_inner, L):
    S = xbc.shape[0]
    G = SSM_GROUPS
    gw = d_inner // G
    hpg = gw // SSM_HEAD_DIM
    b_blk0 = d_inner // SSM_STATE
    c_blk0 = b_blk0 + G
    return pl.pallas_call(
        functools.partial(_ssd_body, L=L, hpg=hpg),
        grid=(S // L, G),
        in_specs=[
            pl.BlockSpec((L, gw), lambda c, g: (c, g)),
            pl.BlockSpec((L, SSM_STATE), lambda c, g: (c, b_blk0 + g)),
            pl.BlockSpec((L, SSM_STATE), lambda c, g: (c, c_blk0 + g)),
            pl.BlockSpec((L, gw), lambda c, g: (c, g)),
            pl.BlockSpec((None, L, LANES), lambda c, g: (g, c, 0)),
            pl.BlockSpec((None, hpg, L), lambda c, g: (g, 0, c)),
            pl.BlockSpec((None, 1, LANES), lambda c, g: (g, 0, 0)),
            pl.BlockSpec((None, hpg, LANES), lambda c, g: (g, 0, 0)),
            pl.BlockSpec((1, gw), lambda c, g: (0, g)),
            pl.BlockSpec((1, gw), lambda c, g: (0, g)),
            pl.BlockSpec((L, L), lambda c, g: (0, 0)),
            pl.BlockSpec((L, L), lambda c, g: (0, 0)),
        ],
        out_specs=pl.BlockSpec((L, gw), lambda c, g: (c, g)),
        out_shape=jax.ShapeDtypeStruct((S, d_inner), BF16),
        scratch_shapes=[pltpu.VMEM((G, SSM_STATE, gw), F32)],
        compiler_params=_params("arbitrary", "arbitrary"),
        name="ssd",
    )(xbc, xbc, xbc, zs, dtc, dtr, a_col, a_row, d_tab, ng, tri_l, tri_u)


def _rope_layout_index():
    x1 = jnp.arange(HALF_ROPE)
    x2 = x1 + HALF_ROPE
    return jnp.concatenate([x1, x2, x2, x1])


def _mla_layer(x, pos, norm_g, w_in, q_lora_g, w_uq, kv_lora_g, w_ukv, qn_g, kn_g, w_o, t):
    S, D = x.shape
    rl = _rope_layout_index()
    lora = Q_LORA_RANK + KV_LORA_RANK
    win_p = jnp.concatenate([w_in[:, :lora], w_in[:, lora + rl]], axis=1).astype(BF16)
    head_cols = jnp.concatenate([jnp.arange(QK_NOPE_DIM), QK_NOPE_DIM + rl])
    uq_cols = (jnp.arange(MLA_HEADS)[:, None] * QK_HEAD_DIM + head_cols[None, :]).reshape(-1)
    wuq_p = w_uq[:, uq_cols].astype(BF16)
    qgain = (qn_g[head_cols] * (QK_HEAD_DIM ** -0.5))[None, :]
    kgain = kn_g[head_cols][None, :]
    inv_freq = ROPE_THETA ** (-jnp.arange(0, QK_ROPE_DIM, 2, dtype=F32) / QK_ROPE_DIM)
    ones = jnp.ones((HALF_ROPE,), F32)
    zeros = jnp.zeros((2 * HALF_ROPE,), F32)
    rope_tab = jnp.zeros((SUBLANES, LANES), F32)
    rope_tab = rope_tab.at[0].set(jnp.tile(inv_freq, 4))
    rope_tab = rope_tab.at[1].set(jnp.concatenate([ones, ones, zeros]))
    rope_tab = rope_tab.at[2].set(jnp.concatenate([-ones, ones, zeros]))

    q, k, v = _mla_proj(x, pos, norm_g[None, :], win_p, q_lora_g[None, :], wuq_p,
                        kv_lora_g[None, :], w_ukv.astype(BF16), qgain, kgain, rope_tab, t["proj_tm"])
    bk = t["attn_bk"]
    vt = v.reshape(S // bk, bk, MLA_HEADS, V_HEAD_DIM).transpose(2, 0, 3, 1)
    o_t = _attention(q, k, vt, t["attn_bq"], bk)
    return _residual_matmul(x, o_t.T, w_o.astype(BF16), t["out_tm"])


def _ssm_layer(x, norm_g, w_in, conv_w, conv_b, dt_bias, a_log, d_skip, out_norm_g, w_out, t):
    S, D = x.shape
    n_heads = dt_bias.shape[0]
    d_inner = n_heads * SSM_HEAD_DIM
    G = SSM_GROUPS
    hpg = n_heads // G
    n_zx = w_in.shape[1] - n_heads
    L = SSM_CHUNK if S % SSM_CHUNK == 0 else S
    pad = LANES - n_heads
    wdt = jnp.pad(w_in[:, n_zx:], ((0, 0), (0, pad))).astype(BF16)
    dtb = jnp.pad(dt_bias, (0, pad))[None, :]
    zs, xbc, dt = _ssm_in(x, norm_g[None, :], w_in[:, :n_zx].astype(BF16), wdt, dtb,
                          conv_w, conv_b[None, :], d_inner, t["ssm_tm"], t["ssm_tn"])
    dt_g = dt[:, :n_heads].reshape(S, G, hpg).transpose(1, 0, 2)
    dtc = jnp.pad(dt_g, ((0, 0), (0, 0), (0, LANES - hpg)))
    dtr = dt_g.transpose(0, 2, 1)
    a = -jnp.exp(a_log).reshape(G, hpg)
    a_col = jnp.pad(a, ((0, 0), (0, LANES - hpg)))[:, None, :]
    a_row = jnp.broadcast_to(a[:, :, None], (G, hpg, LANES))
    d_tab = jnp.repeat(d_skip, SSM_HEAD_DIM)[None, :]
    tri_l = jnp.tril(jnp.ones((L, L), BF16))
    y = _ssd(xbc, zs, dtc, dtr, a_col, a_row, d_tab, out_norm_g[None, :], tri_l, tri_l.T, d_inner, L)
    return _residual_matmul(x, y, w_out.astype(BF16), t["out_tm"])


def kernel(x, positions, mix_norm_g, mlp_norm_g, mlp_w_in, mlp_w_out, mla_w_in, mla_q_norm_g, mla_w_uq, mla_kv_norm_g, mla_w_ukv, mla_qk_norm_q, mla_qk_norm_k, mla_w_o, ssm_w_in, ssm_conv_w, ssm_conv_b, ssm_dt_bias, ssm_a_log, ssm_d, ssm_norm_g, ssm_w_out):
    B, S, D = x.shape
    depth = mix_norm_g.shape[0]
    t = _tiles(S)
    outs = []
    for b in range(B):
        xb = x[b]
        pos = positions[b][:, None]
        for i in range(depth):
            j = i // 2
            if i % 2 == 0:
                xb = _mla_layer(xb, pos, mix_norm_g[i], mla_w_in[j], mla_q_norm_g[j], mla_w_uq[j],
                                mla_kv_norm_g[j], mla_w_ukv[j], mla_qk_norm_q[j], mla_qk_norm_k[j],
                                mla_w_o[j], t)
            else:
                xb = _ssm_layer(xb, mix_norm_g[i], ssm_w_in[j], ssm_conv_w[j], ssm_conv_b[j],
                                ssm_dt_bias[j], ssm_a_log[j], ssm_d[j], ssm_norm_g[j], ssm_w_out[j], t)
            xb = _mlp(xb, mlp_norm_g[i][None, :], mlp_w_in[i].astype(BF16), mlp_w_out[i].astype(BF16),
                      t["mlp_tm"], t["mlp_tf"])
        outs.append(xb)
    return jnp.stack(outs)
```

```python
import functools

import jax
import jax.numpy as jnp
from jax import lax
from jax.experimental import pallas as pl
from jax.experimental.pallas import tpu as pltpu

F32 = jnp.float32
BF16 = jnp.bfloat16

RMS_EPS = 1e-6

MLA_HEADS = 16
QK_NOPE_DIM = 128
QK_ROPE_DIM = 64
QK_HEAD_DIM = QK_NOPE_DIM + QK_ROPE_DIM
V_HEAD_DIM = 128
Q_LORA_RANK = 512
KV_LORA_RANK = 512
ROPE_THETA = 10000.0
SSM_HEAD_DIM = 64
SSM_GROUPS = 8
SSM_STATE = 128
SSM_CONV_WIDTH = 4
SSM_CHUNK = 256

LANES = 128
SUBLANES = 8
VMEM_LIMIT_BYTES = 56 * 1024 * 1024

QK_PAD_DIM = 2 * LANES
HALF_ROPE = QK_ROPE_DIM // 2

MASK_VALUE = -1e30
LOG2_E = 1.4426950408889634


def _tiles(S):
    return dict(
        mlp_tm=min(512, S), mlp_tf=512,
        proj_tm=min(256, S),
        attn_bq=min(1024, S), attn_bk=min(512, S),
        ssm_tm=min(512, S), ssm_tn=1024,
        out_tm=min(512, S),
    )


def _params(*sem):
    return pltpu.CompilerParams(dimension_semantics=sem, vmem_limit_bytes=VMEM_LIMIT_BYTES)


def _rms_rows(x, g):
    ms = jnp.mean(x * x, axis=-1, keepdims=True)
    return x * lax.rsqrt(ms + RMS_EPS) * g


def _resident(shape):
    nd = len(shape)
    return pl.BlockSpec(shape, lambda *_: (0,) * nd, pipeline_mode=pl.Buffered(1))


def _mlp_body(x_ref, g_ref, w1_ref, w2_ref, o_ref, h_ref):
    @pl.when(pl.program_id(1) == 0)
    def _():
        x = x_ref[...]
        h_ref[...] = _rms_rows(x, g_ref[...]).astype(BF16)
        o_ref[...] = x

    u = jnp.dot(h_ref[...], w1_ref[...], preferred_element_type=F32)
    u = jnp.maximum(u, 0.0)
    u = (u * u).astype(BF16)
    o_ref[...] += jnp.dot(u, w2_ref[...], preferred_element_type=F32)


def _mlp(x, g, w1, w2, tm, tf):
    S, D = x.shape
    F = w1.shape[1]
    return pl.pallas_call(
        _mlp_body,
        grid=(S // tm, F // tf),
        in_specs=[
            pl.BlockSpec((tm, D), lambda i, f: (i, 0)),
            pl.BlockSpec((1, D), lambda i, f: (0, 0)),
            pl.BlockSpec((D, tf), lambda i, f: (0, f)),
            pl.BlockSpec((tf, D), lambda i, f: (f, 0)),
        ],
        out_specs=pl.BlockSpec((tm, D), lambda i, f: (i, 0)),
        out_shape=jax.ShapeDtypeStruct((S, D), F32),
        scratch_shapes=[pltpu.VMEM((tm, D), BF16)],
        compiler_params=_params("parallel", "arbitrary"),
        name="mlp",
    )(x, g, w1, w2)


def _resmm_body(x_ref, a_ref, w_ref, o_ref):
    o_ref[...] = x_ref[...] + jnp.dot(a_ref[...], w_ref[...], preferred_element_type=F32)


def _residual_matmul(x, a, w, tm):
    S, D = x.shape
    K = a.shape[1]
    return pl.pallas_call(
        _resmm_body,
        grid=(S // tm,),
        in_specs=[
            pl.BlockSpec((tm, D), lambda i: (i, 0)),
            pl.BlockSpec((tm, K), lambda i: (i, 0)),
            _resident((K, D)),
        ],
        out_specs=pl.BlockSpec((tm, D), lambda i: (i, 0)),
        out_shape=jax.ShapeDtypeStruct((S, D), F32),
        compiler_params=_params("parallel"),
        name="residual_matmul",
    )(x, a, w)


def _mla_proj_body(x_ref, pos_ref, g_ref, win_ref, qlg_ref, wuq_ref, kvlg_ref, wukv_ref,
                   qgain_ref, kgain_ref, rope_ref, q_ref, k_ref, v_ref):
    h = _rms_rows(x_ref[...], g_ref[...]).astype(BF16)
    a = jnp.dot(h, win_ref[...], preferred_element_type=F32)
    cq = _rms_rows(a[:, :Q_LORA_RANK], qlg_ref[...]).astype(BF16)
    ckv = _rms_rows(a[:, Q_LORA_RANK:Q_LORA_RANK + KV_LORA_RANK], kvlg_ref[...]).astype(BF16)
    kr = a[:, Q_LORA_RANK + KV_LORA_RANK:]
    q = jnp.dot(cq, wuq_ref[...], preferred_element_type=F32)
    kv = jnp.dot(ckv, wukv_ref[...], preferred_element_type=F32)

    ang = pos_ref[...].astype(F32) * rope_ref[0:1, :]
    cos_t = jnp.cos(ang) * rope_ref[1:2, :]
    sin_t = jnp.sin(ang) * rope_ref[2:3, :]

    def rope(v):
        return v * cos_t + pltpu.roll(v, 2 * HALF_ROPE, 1) * sin_t

    qgain = qgain_ref[...]
    kgain = kgain_ref[...]
    inv_d = 1.0 / QK_HEAD_DIM
    kr_ss = 0.5 * jnp.sum(kr * kr, axis=-1, keepdims=True)
    kr_rot = rope(kr * kgain[:, LANES:])
    for hd in range(MLA_HEADS):
        lo = hd * QK_PAD_DIM
        mid = lo + LANES
        hi = lo + QK_PAD_DIM
        qn = q[:, lo:mid]
        qr = q[:, mid:hi]
        q_ss = jnp.sum(qn * qn, axis=-1, keepdims=True) + 0.5 * jnp.sum(qr * qr, axis=-1, keepdims=True)
        q_r = lax.rsqrt(q_ss * inv_d + RMS_EPS)
        q_ref[:, lo:mid] = (qn * q_r * qgain[:, :LANES]).astype(BF16)
        q_ref[:, mid:hi] = rope(qr * q_r * qgain[:, LANES:]).astype(BF16)
        kn = kv[:, lo:mid]
        k_ss = jnp.sum(kn * kn, axis=-1, keepdims=True) + kr_ss
        k_r = lax.rsqrt(k_ss * inv_d + RMS_EPS)
        k_ref[:, lo:mid] = (kn * k_r * kgain[:, :LANES]).astype(BF16)
        k_ref[:, mid:hi] = (kr_rot * k_r).astype(BF16)
        v_ref[:, hd * V_HEAD_DIM:(hd + 1) * V_HEAD_DIM] = kv[:, mid:hi].astype(BF16)


def _mla_proj(x, pos, g, win, qlg, wuq, kvlg, wukv, qgain, kgain, rope_tab, tm):
    S, D = x.shape
    HQ = MLA_HEADS * QK_PAD_DIM
    HV = MLA_HEADS * V_HEAD_DIM
    row = lambda w: pl.BlockSpec((tm, w), lambda i: (i, 0))
    return pl.pallas_call(
        _mla_proj_body,
        grid=(S // tm,),
        in_specs=[
            row(D), row(1), _resident(g.shape), _resident(win.shape), _resident(qlg.shape),
            _resident(wuq.shape), _resident(kvlg.shape), _resident(wukv.shape),
            _resident(qgain.shape), _resident(kgain.shape), _resident(rope_tab.shape),
        ],
        out_specs=[row(HQ), row(HQ), row(HV)],
        out_shape=[
            jax.ShapeDtypeStruct((S, HQ), BF16),
            jax.ShapeDtypeStruct((S, HQ), BF16),
            jax.ShapeDtypeStruct((S, HV), BF16),
        ],
        compiler_params=_params("parallel"),
        name="mla_proj",
    )(x, pos, g, win, qlg, wuq, kvlg, wukv, qgain, kgain, rope_tab)


def _attn_body(q_ref, k_ref, vt_ref, o_ref, m_ref, l_ref, acc_ref, s0_ref, s1_ref, *, bq, bk):
    qi = pl.program_id(1)
    q = q_ref[...]
    m_ref[...] = jnp.full(m_ref.shape, MASK_VALUE, F32)
    l_ref[...] = jnp.zeros(l_ref.shape, F32)
    acc_ref[...] = jnp.zeros(acc_ref.shape, F32)
    s_bufs = (s0_ref, s1_ref)

    def scores(kb, s_ref):
        kblk = k_ref[pl.ds(pl.multiple_of(kb * bk, bk), bk), :]
        s_ref[...] = lax.dot_general(kblk, q, (((1,), (1,)), ((), ())), preferred_element_type=F32)

    def update(kb, s_ref, mask):
        s = s_ref[...]
        if mask is not None:
            s = jnp.where(mask, s, MASK_VALUE)
        m_old = m_ref[...]
        m_new = jnp.maximum(m_old, jnp.max(s, axis=0, keepdims=True))
        p = jnp.exp2(s - m_new)
        alpha = jnp.exp2(m_old - m_new)
        l_ref[...] = alpha * l_ref[...] + jnp.sum(p, axis=0, keepdims=True)
        pv = jnp.dot(vt_ref[kb], p.astype(BF16), preferred_element_type=F32)
        acc_ref[...] = alpha * acc_ref[...] + pv
        m_ref[...] = m_new

    per_q = bq // bk
    n_full = qi * per_q
    scores(0, s_bufs[0])

    def full_blocks(t, carry):
        base = t * per_q
        for d in range(per_q):
            scores(base + d + 1, s_bufs[(d + 1) % 2])
            update(base + d, s_bufs[d % 2], None)
        return carry

    lax.fori_loop(0, qi, full_blocks, 0)
    key = lax.broadcasted_iota(jnp.int32, (bk, bq), 0)
    qry = lax.broadcasted_iota(jnp.int32, (bk, bq), 1)
    for d in range(per_q):
        if d + 1 < per_q:
            scores(n_full + d + 1, s_bufs[(d + 1) % 2])
        update(n_full + d, s_bufs[d % 2], key + d * bk <= qry)
    o_ref[...] = (acc_ref[...] / l_ref[...]).astype(o_ref.dtype)


def _attention(q, k, vt, bq, bk):
    S = q.shape[0]
    H = MLA_HEADS
    nkb = S // bk
    assert bq % (2 * bk) == 0 and S % bq == 0
    return pl.pallas_call(
        functools.partial(_attn_body, bq=bq, bk=bk),
        grid=(H, S // bq),
        in_specs=[
            pl.BlockSpec((bq, QK_PAD_DIM), lambda h, i: (i, h)),
            pl.BlockSpec((S, QK_PAD_DIM), lambda h, i: (0, h)),
            pl.BlockSpec((None, nkb, V_HEAD_DIM, bk), lambda h, i: (h, 0, 0, 0)),
        ],
        out_specs=pl.BlockSpec((V_HEAD_DIM, bq), lambda h, i: (h, i)),
        out_shape=jax.ShapeDtypeStruct((H * V_HEAD_DIM, S), BF16),
        scratch_shapes=[
            pltpu.VMEM((1, bq), F32),
            pltpu.VMEM((1, bq), F32),
            pltpu.VMEM((V_HEAD_DIM, bq), F32),
            pltpu.VMEM((bk, bq), F32),
            pltpu.VMEM((bk, bq), F32),
        ],
        compiler_params=_params("parallel", "arbitrary"),
        name="attention",
    )(q, k, vt)


def _silu(x):
    return x * jax.nn.sigmoid(x)


def _ssm_in_body(x_ref, g_ref, w_ref, wdt_ref, dtb_ref, cw_ref, cb_ref,
                 z_ref, xbc_ref, dt_ref, h_ref, buf_ref, carry_ref, *, nz, tm):
    i = pl.program_id(0)
    n = pl.program_id(1)

    @pl.when(n == 0)
    def _():
        h = _rms_rows(x_ref[...], g_ref[...]).astype(BF16)
        h_ref[...] = h
        dt_raw = jnp.dot(h, wdt_ref[...], preferred_element_type=F32)
        dt_ref[...] = jax.nn.softplus(dt_raw + dtb_ref[...])

    pre = jnp.dot(h_ref[...], w_ref[...], preferred_element_type=F32)

    @pl.when(n < nz)
    def _():
        z_ref[...] = _silu(pre)

    @pl.when(n >= nz)
    def _():
        j = n - nz

        @pl.when(i == 0)
        def _():
            carry_ref[j] = jnp.zeros(carry_ref.shape[1:], F32)

        buf_ref[0:SUBLANES, :] = carry_ref[j]
        buf_ref[SUBLANES:, :] = pre
        carry_ref[j] = pre[tm - SUBLANES:, :]
        cw = cw_ref[...]
        acc = cb_ref[...] + cw[SSM_CONV_WIDTH - 1:SSM_CONV_WIDTH, :] * pre
        for kk in range(SSM_CONV_WIDTH - 1):
            shift = SSM_CONV_WIDTH - 1 - kk
            acc = acc + cw[kk:kk + 1, :] * buf_ref[pl.ds(SUBLANES - shift, tm), :]
        xbc_ref[...] = _silu(acc)


def _ssm_in(x, g, w, wdt, dtb, cw, cb, d_inner, tm, tn):
    S, D = x.shape
    N = w.shape[1]
    nz = d_inner // tn
    nx = (N - d_inner) // tn
    zmap = lambda i, n: (i, jnp.minimum(n, nz - 1))
    xmap = lambda i, n: (i, jnp.maximum(n - nz, 0))
    cmap = lambda i, n: (0, jnp.maximum(n - nz, 0))
    return pl.pallas_call(
        functools.partial(_ssm_in_body, nz=nz, tm=tm),
        grid=(S // tm, nz + nx),
        in_specs=[
            pl.BlockSpec((tm, D), lambda i, n: (i, 0)),
            pl.BlockSpec((1, D), lambda i, n: (0, 0)),
            pl.BlockSpec((D, tn), lambda i, n: (0, n)),
            pl.BlockSpec((D, LANES), lambda i, n: (0, 0)),
            pl.BlockSpec((1, LANES), lambda i, n: (0, 0)),
            pl.BlockSpec((SSM_CONV_WIDTH, tn), cmap),
            pl.BlockSpec((1, tn), cmap),
        ],
        out_specs=[
            pl.BlockSpec((tm, tn), zmap),
            pl.BlockSpec((tm, tn), xmap),
            pl.BlockSpec((tm, LANES), lambda i, n: (i, 0)),
        ],
        out_shape=[
            jax.ShapeDtypeStruct((S, d_inner), F32),
            jax.ShapeDtypeStruct((S, N - d_inner), F32),
            jax.ShapeDtypeStruct((S, LANES), F32),
        ],
        scratch_shapes=[
            pltpu.VMEM((tm, D), BF16),
            pltpu.VMEM((tm + SUBLANES, tn), F32),
            pltpu.VMEM((nx, SUBLANES, tn), F32),
        ],
        compiler_params=_params("arbitrary", "arbitrary"),
        name="ssm_in",
    )(x, g, w, wdt, dtb, cw, cb)


def _split3(a):
    hi = a.astype(BF16)
    r1 = a - hi.astype(F32)
    mid = r1.astype(BF16)
    lo = (r1 - mid.astype(F32)).astype(BF16)
    return hi, mid, lo


def _ssd_body(xs_ref, b_ref, c_ref, z_ref, dtc_ref, dtr_ref, ac_ref, ar_ref, d_ref, ng_ref,
              tl_ref, tu_ref, y_ref, st_ref, *, L, hpg):
    c = pl.program_id(0)
    g = pl.program_id(1)

    @pl.when(c == 0)
    def _():
        st_ref[g] = jnp.zeros(st_ref.shape[1:], F32)

    xs = xs_ref[...]
    bmat = b_ref[...]
    cmat = c_ref[...].astype(BF16)
    dt = dtc_ref[...]
    la = dt * ac_ref[...]
    la_t = dtr_ref[...] * ar_ref[:, 0:1]
    cum = sum(jnp.dot(tl_ref[...], p, preferred_element_type=F32) for p in _split3(la))
    cum_t = sum(jnp.dot(p, tu_ref[...], preferred_element_type=F32) for p in _split3(la_t))

    cb = lax.dot_general(cmat, bmat.astype(BF16), (((1,), (1,)), ((), ())),
                         preferred_element_type=F32)
    t_out = lax.broadcasted_iota(jnp.int32, (L, L), 0)
    t_in = lax.broadcasted_iota(jnp.int32, (L, L), 1)
    causal = t_in <= t_out
    first_head = lax.broadcasted_iota(jnp.int32, (L, LANES), 1) < SSM_HEAD_DIM

    state = st_ref[g]
    y_off = jnp.dot(cmat, state.astype(BF16), preferred_element_type=F32)
    cum_last = cum[L - 1:L, :]

    def col(a, j, rows):
        return jnp.broadcast_to(a[:, j:j + 1], (rows, LANES))

    ys, xws, cds = [], [], []
    for pr in range(hpg // 2):
        j0, j1 = 2 * pr, 2 * pr + 1
        sl = slice(pr * LANES, (pr + 1) * LANES)
        cum_p = jnp.where(first_head, col(cum, j0, L), col(cum, j1, L))
        dt_p = jnp.where(first_head, col(dt, j0, L), col(dt, j1, L))
        last_p = jnp.where(first_head[:1], col(cum_last, j0, 1), col(cum_last, j1, 1))
        xdt = xs[:, sl] * dt_p
        ms = []
        for j in (j0, j1):
            diff = jnp.broadcast_to(cum[:, j:j + 1], (L, L)) - cum_t[j:j + 1, :]
            decay = jnp.exp(jnp.where(causal, diff, -jnp.inf))
            ms.append((cb * decay).astype(BF16))
        lhs = jnp.concatenate(ms, axis=1)
        rhs = jnp.concatenate([jnp.where(first_head, xdt, 0.0).astype(BF16),
                               jnp.where(first_head, 0.0, xdt).astype(BF16)], axis=0)
        y_diag = jnp.dot(lhs, rhs, preferred_element_type=F32)
        ys.append(y_diag + y_off[:, sl] * jnp.exp(cum_p))
        xws.append((xdt * jnp.exp(last_p - cum_p)).astype(BF16))
        cds.append(jnp.exp(last_p))

    xw = jnp.concatenate(xws, axis=1)
    chunk_decay = jnp.concatenate(cds, axis=1)
    st_ref[g] = state * chunk_decay + jnp.dot(bmat.T.astype(BF16), xw, preferred_element_type=F32)

    y = jnp.concatenate(ys, axis=1) + d_ref[...] * xs
    y = y * z_ref[...]
    y = y * lax.rsqrt(jnp.mean(y * y, axis=-1, keepdims=True) + RMS_EPS)
    y_ref[...] = (y * ng_ref[...]).astype(BF16)


def _ssd(xbc, zs, dtc, dtr, a_col, a_row, d_tab, ng, tri_l, tri_u, d_inner, L):
    S = xbc.shape[0]
    G = SSM_GROUPS
    gw = d_inner // G
    hpg = gw // SSM_HEAD_DIM
    b_blk0 = d_inner // SSM_STATE
    c_blk0 = b_blk0 + G
    return pl.pallas_call(
        functools.partial(_ssd_body, L=L, hpg=hpg),
        grid=(S // L, G),
        in_specs=[
            pl.BlockSpec((L, gw), lambda c, g: (c, g)),
            pl.BlockSpec((L, SSM_STATE), lambda c, g: (c, b_blk0 + g)),
            pl.BlockSpec((L, SSM_STATE), lambda c, g: (c, c_blk0 + g)),
            pl.BlockSpec((L, gw), lambda c, g: (c, g)),
            pl.BlockSpec((None, L, LANES), lambda c, g: (g, c, 0)),
            pl.BlockSpec((None, hpg, L), lambda c, g: (g, 0, c)),
            pl.BlockSpec((None, 1, LANES), lambda c, g: (g, 0, 0)),
            pl.BlockSpec((None, hpg, LANES), lambda c, g: (g, 0, 0)),
            pl.BlockSpec((1, gw), lambda c, g: (0, g)),
            pl.BlockSpec((1, gw), lambda c, g: (0, g)),
            pl.BlockSpec((L, L), lambda c, g: (0, 0)),
            pl.BlockSpec((L, L), lambda c, g: (0, 0)),
        ],
        out_specs=pl.BlockSpec((L, gw), lambda c, g: (c, g)),
        out_shape=jax.ShapeDtypeStruct((S, d_inner), BF16),
        scratch_shapes=[pltpu.VMEM((G, SSM_STATE, gw), F32)],
        compiler_params=_params("arbitrary", "arbitrary"),
        name="ssd",
    )(xbc, xbc, xbc, zs, dtc, dtr, a_col, a_row, d_tab, ng, tri_l, tri_u)


def _rope_layout_index():
    x1 = jnp.arange(HALF_ROPE)
    x2 = x1 + HALF_ROPE
    return jnp.concatenate([x1, x2, x2, x1])


def _mla_layer(x, pos, norm_g, w_in, q_lora_g, w_uq, kv_lora_g, w_ukv, qn_g, kn_g, w_o, t):
    S, D = x.shape
    rl = _rope_layout_index()
    lora = Q_LORA_RANK + KV_LORA_RANK
    win_p = jnp.concatenate([w_in[:, :lora], w_in[:, lora + rl]], axis=1).astype(BF16)
    head_cols = jnp.concatenate([jnp.arange(QK_NOPE_DIM), QK_NOPE_DIM + rl])
    uq_cols = (jnp.arange(MLA_HEADS)[:, None] * QK_HEAD_DIM + head_cols[None, :]).reshape(-1)
    wuq_p = w_uq[:, uq_cols].astype(BF16)
    qgain = (qn_g[head_cols] * (QK_HEAD_DIM ** -0.5 * LOG2_E))[None, :]
    kgain = kn_g[head_cols][None, :]
    inv_freq = ROPE_THETA ** (-jnp.arange(0, QK_ROPE_DIM, 2, dtype=F32) / QK_ROPE_DIM)
    ones = jnp.ones((HALF_ROPE,), F32)
    zeros = jnp.zeros((2 * HALF_ROPE,), F32)
    rope_tab = jnp.zeros((SUBLANES, LANES), F32)
    rope_tab = rope_tab.at[0].set(jnp.tile(inv_freq, 4))
    rope_tab = rope_tab.at[1].set(jnp.concatenate([ones, ones, zeros]))
    rope_tab = rope_tab.at[2].set(jnp.concatenate([-ones, ones, zeros]))

    q, k, v = _mla_proj(x, pos, norm_g[None, :], win_p, q_lora_g[None, :], wuq_p,
                        kv_lora_g[None, :], w_ukv.astype(BF16), qgain, kgain, rope_tab, t["proj_tm"])
    bk = t["attn_bk"]
    vt = v.reshape(S // bk, bk, MLA_HEADS, V_HEAD_DIM).transpose(2, 0, 3, 1)
    o_t = _attention(q, k, vt, t["attn_bq"], bk)
    return _residual_matmul(x, o_t.T, w_o.astype(BF16), t["out_tm"])


def _ssm_layer(x, norm_g, w_in, conv_w, conv_b, dt_bias, a_log, d_skip, out_norm_g, w_out, t):
    S, D = x.shape
    n_heads = dt_bias.shape[0]
    d_inner = n_heads * SSM_HEAD_DIM
    G = SSM_GROUPS
    hpg = n_heads // G
    n_zx = w_in.shape[1] - n_heads
    L = SSM_CHUNK if S % SSM_CHUNK == 0 else S
    pad = LANES - n_heads
    wdt = jnp.pad(w_in[:, n_zx:], ((0, 0), (0, pad))).astype(BF16)
    dtb = jnp.pad(dt_bias, (0, pad))[None, :]
    zs, xbc, dt = _ssm_in(x, norm_g[None, :], w_in[:, :n_zx].astype(BF16), wdt, dtb,
                          conv_w, conv_b[None, :], d_inner, t["ssm_tm"], t["ssm_tn"])
    dt_g = dt[:, :n_heads].reshape(S, G, hpg).transpose(1, 0, 2)
    dtc = jnp.pad(dt_g, ((0, 0), (0, 0), (0, LANES - hpg)))
    dtr = dt_g.transpose(0, 2, 1)
    a = -jnp.exp(a_log).reshape(G, hpg)
    a_col = jnp.pad(a, ((0, 0), (0, LANES - hpg)))[:, None, :]
    a_row = jnp.broadcast_to(a[:, :, None], (G, hpg, LANES))
    d_tab = jnp.repeat(d_skip, SSM_HEAD_DIM)[None, :]
    tri_l = jnp.tril(jnp.ones((L, L), BF16))
    y = _ssd(xbc, zs, dtc, dtr, a_col, a_row, d_tab, out_norm_g[None, :], tri_l, tri_l.T, d_inner, L)
    return _residual_matmul(x, y, w_out.astype(BF16), t["out_tm"])


def kernel(x, positions, mix_norm_g, mlp_norm_g, mlp_w_in, mlp_w_out, mla_w_in, mla_q_norm_g, mla_w_uq, mla_kv_norm_g, mla_w_ukv, mla_qk_norm_q, mla_qk_norm_k, mla_w_o, ssm_w_in, ssm_conv_w, ssm_conv_b, ssm_dt_bias, ssm_a_log, ssm_d, ssm_norm_g, ssm_w_out):
    B, S, D = x.shape
    depth = mix_norm_g.shape[0]
    t = _tiles(S)
    outs = []
    for b in range(B):
        xb = x[b]
        pos = positions[b][:, None]
        for i in range(depth):
            j = i // 2
            if i % 2 == 0:
                xb = _mla_layer(xb, pos, mix_norm_g[i], mla_w_in[j], mla_q_norm_g[j], mla_w_uq[j],
                                mla_kv_norm_g[j], mla_w_ukv[j], mla_qk_norm_q[j], mla_qk_norm_k[j],
                                mla_w_o[j], t)
            else:
                xb = _ssm_layer(xb, mix_norm_g[i], ssm_w_in[j], ssm_conv_w[j], ssm_conv_b[j],
                                ssm_dt_bias[j], ssm_a_log[j], ssm_d[j], ssm_norm_g[j], ssm_w_out[j], t)
            xb = _mlp(xb, mlp_norm_g[i][None, :], mlp_w_in[i].astype(BF16), mlp_w_out[i].astype(BF16),
                      t["mlp_tm"], t["mlp_tf"])
        outs.append(xb)
    return jnp.stack(outs)
```

```python
import functools

import jax
import jax.numpy as jnp
from jax import lax
from jax.experimental import pallas as pl
from jax.experimental.pallas import tpu as pltpu

F32 = jnp.float32
BF16 = jnp.bfloat16

RMS_EPS = 1e-6

MLA_HEADS = 16
QK_NOPE_DIM = 128
QK_ROPE_DIM = 64
QK_HEAD_DIM = QK_NOPE_DIM + QK_ROPE_DIM
V_HEAD_DIM = 128
Q_LORA_RANK = 512
KV_LORA_RANK = 512
ROPE_THETA = 10000.0
SSM_HEAD_DIM = 64
SSM_GROUPS = 8
SSM_STATE = 128
SSM_CONV_WIDTH = 4
SSM_CHUNK = 256

LANES = 128
SUBLANES = 8
VMEM_LIMIT_BYTES = 56 * 1024 * 1024

QK_PAD_DIM = 2 * LANES
HALF_ROPE = QK_ROPE_DIM // 2

MASK_VALUE = -1e30
LOG2_E = 1.4426950408889634


def _tiles(S):
    return dict(
        mlp_tm=min(1024, S), mlp_tf=512,
        proj_tm=min(256, S),
        attn_bq=min(1024, S), attn_bk=min(512, S),
        ssm_tm=min(512, S), ssm_tn=1024,
        out_tm=min(512, S),
    )


def _params(*sem):
    return pltpu.CompilerParams(dimension_semantics=sem, vmem_limit_bytes=VMEM_LIMIT_BYTES)


def _rms_rows(x, g):
    ms = jnp.mean(x * x, axis=-1, keepdims=True)
    return x * lax.rsqrt(ms + RMS_EPS) * g


def _resident(shape):
    nd = len(shape)
    return pl.BlockSpec(shape, lambda *_: (0,) * nd, pipeline_mode=pl.Buffered(1))


def _mlp_body(x_ref, g_ref, w1_ref, w2_ref, o_ref, h_ref):
    @pl.when(pl.program_id(1) == 0)
    def _():
        x = x_ref[...]
        h_ref[...] = _rms_rows(x, g_ref[...]).astype(BF16)
        o_ref[...] = x

    u = jnp.dot(h_ref[...], w1_ref[...], preferred_element_type=F32)
    u = jnp.maximum(u, 0.0)
    u = (u * u).astype(BF16)
    o_ref[...] += jnp.dot(u, w2_ref[...], preferred_element_type=F32)


def _mlp(x, g, w1, w2, tm, tf):
    S, D = x.shape
    F = w1.shape[1]
    return pl.pallas_call(
        _mlp_body,
        grid=(S // tm, F // tf),
        in_specs=[
            pl.BlockSpec((tm, D), lambda i, f: (i, 0)),
            pl.BlockSpec((1, D), lambda i, f: (0, 0)),
            pl.BlockSpec((D, tf), lambda i, f: (0, f)),
            pl.BlockSpec((tf, D), lambda i, f: (f, 0)),
        ],
        out_specs=pl.BlockSpec((tm, D), lambda i, f: (i, 0)),
        out_shape=jax.ShapeDtypeStruct((S, D), F32),
        scratch_shapes=[pltpu.VMEM((tm, D), BF16)],
        compiler_params=_params("parallel", "arbitrary"),
        name="mlp",
    )(x, g, w1, w2)


def _resmm_body(x_ref, a_ref, w_ref, o_ref):
    o_ref[...] = x_ref[...] + jnp.dot(a_ref[...], w_ref[...], preferred_element_type=F32)


def _residual_matmul(x, a, w, tm):
    S, D = x.shape
    K = a.shape[1]
    return pl.pallas_call(
        _resmm_body,
        grid=(S // tm,),
        in_specs=[
            pl.BlockSpec((tm, D), lambda i: (i, 0)),
            pl.BlockSpec((tm, K), lambda i: (i, 0)),
            _resident((K, D)),
        ],
        out_specs=pl.BlockSpec((tm, D), lambda i: (i, 0)),
        out_shape=jax.ShapeDtypeStruct((S, D), F32),
        compiler_params=_params("parallel"),
        name="residual_matmul",
    )(x, a, w)


def _mla_proj_body(x_ref, pos_ref, g_ref, win_ref, qlg_ref, wuq_ref, kvlg_ref, wukv_ref,
                   qgain_ref, kgain_ref, rope_ref, q_ref, k_ref, v_ref):
    h = _rms_rows(x_ref[...], g_ref[...]).astype(BF16)
    a = jnp.dot(h, win_ref[...], preferred_element_type=F32)
    cq = _rms_rows(a[:, :Q_LORA_RANK], qlg_ref[...]).astype(BF16)
    ckv = _rms_rows(a[:, Q_LORA_RANK:Q_LORA_RANK + KV_LORA_RANK], kvlg_ref[...]).astype(BF16)
    kr = a[:, Q_LORA_RANK + KV_LORA_RANK:]
    q = jnp.dot(cq, wuq_ref[...], preferred_element_type=F32)
    kv = jnp.dot(ckv, wukv_ref[...], preferred_element_type=F32)

    ang = pos_ref[...].astype(F32) * rope_ref[0:1, :]
    cos_t = jnp.cos(ang) * rope_ref[1:2, :]
    sin_t = jnp.sin(ang) * rope_ref[2:3, :]

    def rope(v):
        return v * cos_t + pltpu.roll(v, 2 * HALF_ROPE, 1) * sin_t

    qgain = qgain_ref[...]
    kgain = kgain_ref[...]
    inv_d = 1.0 / QK_HEAD_DIM
    kr_ss = 0.5 * jnp.sum(kr * kr, axis=-1, keepdims=True)
    kr_rot = rope(kr * kgain[:, LANES:])
    for hd in range(MLA_HEADS):
        lo = hd * QK_PAD_DIM
        mid = lo + LANES
        hi = lo + QK_PAD_DIM
        qn = q[:, lo:mid]
        qr = q[:, mid:hi]
        q_ss = jnp.sum(qn * qn, axis=-1, keepdims=True) + 0.5 * jnp.sum(qr * qr, axis=-1, keepdims=True)
        q_r = lax.rsqrt(q_ss * inv_d + RMS_EPS)
        q_ref[:, lo:mid] = (qn * q_r * qgain[:, :LANES]).astype(BF16)
        q_ref[:, mid:hi] = rope(qr * q_r * qgain[:, LANES:]).astype(BF16)
        kn = kv[:, lo:mid]
        k_ss = jnp.sum(kn * kn, axis=-1, keepdims=True) + kr_ss
        k_r = lax.rsqrt(k_ss * inv_d + RMS_EPS)
        k_ref[:, lo:mid] = (kn * k_r * kgain[:, :LANES]).astype(BF16)
        k_ref[:, mid:hi] = (kr_rot * k_r).astype(BF16)
        v_ref[:, hd * V_HEAD_DIM:(hd + 1) * V_HEAD_DIM] = kv[:, mid:hi].astype(BF16)


def _mla_proj(x, pos, g, win, qlg, wuq, kvlg, wukv, qgain, kgain, rope_tab, tm):
    S, D = x.shape
    HQ = MLA_HEADS * QK_PAD_DIM
    HV = MLA_HEADS * V_HEAD_DIM
    row = lambda w: pl.BlockSpec((tm, w), lambda i: (i, 0))
    return pl.pallas_call(
        _mla_proj_body,
        grid=(S // tm,),
        in_specs=[
            row(D), row(1), _resident(g.shape), _resident(win.shape), _resident(qlg.shape),
            _resident(wuq.shape), _resident(kvlg.shape), _resident(wukv.shape),
            _resident(qgain.shape), _resident(kgain.shape), _resident(rope_tab.shape),
        ],
        out_specs=[row(HQ), row(HQ), row(HV)],
        out_shape=[
            jax.ShapeDtypeStruct((S, HQ), BF16),
            jax.ShapeDtypeStruct((S, HQ), BF16),
            jax.ShapeDtypeStruct((S, HV), BF16),
        ],
        compiler_params=_params("parallel"),
        name="mla_proj",
    )(x, pos, g, win, qlg, wuq, kvlg, wukv, qgain, kgain, rope_tab)


def _attn_body(qt_ref, k_ref, vt_ref, o_ref, m_ref, l_ref, acc_ref, s0_ref, s1_ref, *, bq, bk):
    qi = pl.program_id(1)
    qt = qt_ref[...]
    m_ref[...] = jnp.full(m_ref.shape, MASK_VALUE, F32)
    l_ref[...] = jnp.zeros(l_ref.shape, F32)
    acc_ref[...] = jnp.zeros(acc_ref.shape, F32)
    s_bufs = (s0_ref, s1_ref)

    def scores(kb, s_ref, lo):
        kblk = k_ref[pl.ds(pl.multiple_of(kb * bk, bk), bk), :]
        s_ref[:, lo:] = jnp.dot(kblk, qt[:, lo:], preferred_element_type=F32)

    def update(kb, s_ref, lo, diagonal):
        s = s_ref[:, lo:]
        if diagonal:
            key = lax.broadcasted_iota(jnp.int32, s.shape, 0)
            qry = lax.broadcasted_iota(jnp.int32, s.shape, 1)
            s = jnp.where(key <= qry, s, MASK_VALUE)
        m_old = m_ref[:, lo:]
        m_new = jnp.maximum(m_old, jnp.max(s, axis=0, keepdims=True))
        p = jnp.exp2(s - m_new)
        alpha = jnp.exp2(m_old - m_new)
        l_ref[:, lo:] = alpha * l_ref[:, lo:] + jnp.sum(p, axis=0, keepdims=True)
        pv = jnp.dot(vt_ref[kb], p.astype(BF16), preferred_element_type=F32)
        acc_ref[:, lo:] = alpha * acc_ref[:, lo:] + pv
        m_ref[:, lo:] = m_new

    per_q = bq // bk
    n_full = qi * per_q
    scores(0, s_bufs[0], 0)

    def full_blocks(t, carry):
        base = t * per_q
        for d in range(per_q):
            scores(base + d + 1, s_bufs[(d + 1) % 2], 0)
            update(base + d, s_bufs[d % 2], 0, False)
        return carry

    lax.fori_loop(0, qi, full_blocks, 0)
    for d in range(per_q):
        if d + 1 < per_q:
            scores(n_full + d + 1, s_bufs[(d + 1) % 2], (d + 1) * bk)
        update(n_full + d, s_bufs[d % 2], d * bk, True)
    o_ref[...] = (acc_ref[...] / l_ref[...]).astype(o_ref.dtype)


def _attention(qt, k, vt, bq, bk):
    S = k.shape[0]
    H = MLA_HEADS
    nkb = S // bk
    assert bq % (2 * bk) == 0 and S % bq == 0
    return pl.pallas_call(
        functools.partial(_attn_body, bq=bq, bk=bk),
        grid=(H, S // bq),
        in_specs=[
            pl.BlockSpec((QK_PAD_DIM, bq), lambda h, i: (h, i)),
            pl.BlockSpec((S, QK_PAD_DIM), lambda h, i: (0, h)),
            pl.BlockSpec((None, nkb, V_HEAD_DIM, bk), lambda h, i: (h, 0, 0, 0)),
        ],
        out_specs=pl.BlockSpec((V_HEAD_DIM, bq), lambda h, i: (h, i)),
        out_shape=jax.ShapeDtypeStruct((H * V_HEAD_DIM, S), BF16),
        scratch_shapes=[
            pltpu.VMEM((1, bq), F32),
            pltpu.VMEM((1, bq), F32),
            pltpu.VMEM((V_HEAD_DIM, bq), F32),
            pltpu.VMEM((bk, bq), F32),
            pltpu.VMEM((bk, bq), F32),
        ],
        compiler_params=_params("parallel", "arbitrary"),
        name="attention",
    )(qt, k, vt)


def _silu(x):
    return x * jax.nn.sigmoid(x)


SSM_SUB_TILE = 2 * LANES


def _ssm_z_body(x_ref, g_ref, w_ref, wdt_ref, dtb_ref, z_ref, dt_ref, h_ref):
    @pl.when(pl.program_id(1) == 0)
    def _():
        h = _rms_rows(x_ref[...], g_ref[...]).astype(BF16)
        h_ref[...] = h
        dt_raw = jnp.dot(h, wdt_ref[...], preferred_element_type=F32)
        dt_ref[...] = jax.nn.softplus(dt_raw + dtb_ref[...])

    h = h_ref[...]
    for c0 in range(0, z_ref.shape[1], SSM_SUB_TILE):
        cs = slice(c0, c0 + SSM_SUB_TILE)
        z_ref[:, cs] = jnp.dot(h, w_ref[:, cs], preferred_element_type=F32)
    for c0 in range(0, z_ref.shape[1], SSM_SUB_TILE):
        cs = slice(c0, c0 + SSM_SUB_TILE)
        z_ref[:, cs] = _silu(z_ref[:, cs])


def _ssm_xbc_body(x_ref, g_ref, w_ref, cw_ref, cb_ref, o_ref, h_ref, buf_ref, carry_ref):
    i = pl.program_id(0)
    n = pl.program_id(1)
    tm = o_ref.shape[0]

    @pl.when(n == 0)
    def _():
        h_ref[...] = _rms_rows(x_ref[...], g_ref[...]).astype(BF16)

    @pl.when(i == 0)
    def _():
        carry_ref[n] = jnp.zeros(carry_ref.shape[1:], F32)

    h = h_ref[...]
    buf_ref[0:SUBLANES, :] = carry_ref[n]
    def project(cs):
        buf_ref[SUBLANES:, cs] = jnp.dot(h, w_ref[:, cs], preferred_element_type=F32)

    def conv_act(cs):
        full = buf_ref[:, cs]
        pre = full[SUBLANES:, :]
        cw = cw_ref[:, cs]
        acc = cb_ref[:, cs] + cw[SSM_CONV_WIDTH - 1:SSM_CONV_WIDTH, :] * pre
        for kk in range(SSM_CONV_WIDTH - 1):
            shift = SSM_CONV_WIDTH - 1 - kk
            acc = acc + cw[kk:kk + 1, :] * pltpu.roll(full, shift, 0)[SUBLANES:, :]
        o_ref[:, cs] = _silu(acc)

    subs = [slice(c0, c0 + SSM_SUB_TILE) for c0 in range(0, o_ref.shape[1], SSM_SUB_TILE)]
    project(subs[0])
    for prev, cur in zip(subs[:-1], subs[1:]):
        project(cur)
        conv_act(prev)
    conv_act(subs[-1])
    carry_ref[n] = buf_ref[tm:, :]


def _ssm_in(x, g, w, wdt, dtb, cw, cb, d_inner, tm, tn):
    S, D = x.shape
    N = w.shape[1]
    nz = d_inner // tn
    nx = (N - d_inner) // tn
    x_spec = pl.BlockSpec((tm, D), lambda i, n: (i, 0))
    g_spec = pl.BlockSpec((1, D), lambda i, n: (0, 0))
    zs, dt = pl.pallas_call(
        _ssm_z_body,
        grid=(S // tm, nz),
        in_specs=[
            x_spec, g_spec,
            pl.BlockSpec((D, tn), lambda i, n: (0, n)),
            pl.BlockSpec((D, LANES), lambda i, n: (0, 0)),
            pl.BlockSpec((1, LANES), lambda i, n: (0, 0)),
        ],
        out_specs=[
            pl.BlockSpec((tm, tn), lambda i, n: (i, n)),
            pl.BlockSpec((tm, LANES), lambda i, n: (i, 0)),
        ],
        out_shape=[
            jax.ShapeDtypeStruct((S, d_inner), F32),
            jax.ShapeDtypeStruct((S, LANES), F32),
        ],
        scratch_shapes=[pltpu.VMEM((tm, D), BF16)],
        compiler_params=_params("parallel", "arbitrary"),
        name="ssm_in_z",
    )(x, g, w, wdt, dtb)
    xbc = pl.pallas_call(
        _ssm_xbc_body,
        grid=(S // tm, nx),
        in_specs=[
            x_spec, g_spec,
            pl.BlockSpec((D, tn), lambda i, n: (0, nz + n)),
            pl.BlockSpec((SSM_CONV_WIDTH, tn), lambda i, n: (0, n)),
            pl.BlockSpec((1, tn), lambda i, n: (0, n)),
        ],
        out_specs=pl.BlockSpec((tm, tn), lambda i, n: (i, n)),
        out_shape=jax.ShapeDtypeStruct((S, N - d_inner), F32),
        scratch_shapes=[
            pltpu.VMEM((tm, D), BF16),
            pltpu.VMEM((tm + SUBLANES, tn), F32),
            pltpu.VMEM((nx, SUBLANES, tn), F32),
        ],
        compiler_params=_params("arbitrary", "arbitrary"),
        name="ssm_in_xbc",
    )(x, g, w, cw, cb)
    return zs, xbc, dt


def _split3(a):
    hi = a.astype(BF16)
    r1 = a - hi.astype(F32)
    mid = r1.astype(BF16)
    lo = (r1 - mid.astype(F32)).astype(BF16)
    return hi, mid, lo


def _ssd_body(xs_ref, b_ref, c_ref, z_ref, dtc_ref, dtr_ref, ac_ref, ar_ref, d_ref, ng_ref,
              tl_ref, tu_ref, y_ref, st_ref, *, L, hpg):
    c = pl.program_id(0)
    g = pl.program_id(1)

    @pl.when(c == 0)
    def _():
        st_ref[g] = jnp.zeros(st_ref.shape[1:], F32)

    xs = xs_ref[...]
    bmat = b_ref[...]
    cmat = c_ref[...].astype(BF16)
    dt = dtc_ref[...]
    la = dt * ac_ref[...]
    la_t = dtr_ref[...] * ar_ref[:, 0:1]
    cum = sum(jnp.dot(tl_ref[...], p, preferred_element_type=F32) for p in _split3(la))
    cum_t = sum(jnp.dot(p, tu_ref[...], preferred_element_type=F32) for p in _split3(la_t))

    cb = lax.dot_general(cmat, bmat.astype(BF16), (((1,), (1,)), ((), ())),
                         preferred_element_type=F32)
    half = LANES
    t_out = lax.broadcasted_iota(jnp.int32, (half, half), 0)
    t_in = lax.broadcasted_iota(jnp.int32, (half, half), 1)
    causal = t_in <= t_out
    first_head = lax.broadcasted_iota(jnp.int32, (L, LANES), 1) < SSM_HEAD_DIM
    zero_q = jnp.zeros((half, half), F32)

    state = st_ref[g]
    y_off = jnp.dot(cmat, state.astype(BF16), preferred_element_type=F32)
    cum_last = cum[L - 1:L, :]

    def col(a, j, rows):
        return jnp.broadcast_to(a[:, j:j + 1], (rows, LANES))

    ys, xws, cds = [], [], []
    for pr in range(hpg // 2):
        j0, j1 = 2 * pr, 2 * pr + 1
        sl = slice(pr * LANES, (pr + 1) * LANES)
        cum_b = (col(cum, j0, L), col(cum, j1, L))
        cum_p = jnp.where(first_head, cum_b[0], cum_b[1])
        dt_p = jnp.where(first_head, col(dt, j0, L), col(dt, j1, L))
        last_p = jnp.where(first_head[:1], col(cum_last, j0, 1), col(cum_last, j1, 1))
        xdt = xs[:, sl] * dt_p
        ms = []
        for cb_j, j in zip(cum_b, (j0, j1)):
            row = cum_t[j:j + 1, :]
            d_tl = jnp.exp(jnp.where(causal, cb_j[:half] - row[:, :half], -jnp.inf))
            d_bl = jnp.exp(cb_j[half:] - row[:, :half])
            d_br = jnp.exp(jnp.where(causal, cb_j[half:] - row[:, half:], -jnp.inf))
            top = jnp.concatenate([cb[:half, :half] * d_tl, zero_q], axis=1)
            bot = jnp.concatenate([cb[half:, :half] * d_bl, cb[half:, half:] * d_br], axis=1)
            ms.append(jnp.concatenate([top, bot], axis=0).astype(BF16))
        lhs = jnp.concatenate(ms, axis=1)
        rhs = jnp.concatenate([jnp.where(first_head, xdt, 0.0).astype(BF16),
                               jnp.where(first_head, 0.0, xdt).astype(BF16)], axis=0)
        y_diag = jnp.dot(lhs, rhs, preferred_element_type=F32)
        ys.append(y_diag + y_off[:, sl] * jnp.exp(cum_p))
        xws.append((xdt * jnp.exp(last_p - cum_p)).astype(BF16))
        cds.append(jnp.exp(last_p))

    xw = jnp.concatenate(xws, axis=1)
    chunk_decay = jnp.concatenate(cds, axis=1)
    st_ref[g] = state * chunk_decay + jnp.dot(bmat.T.astype(BF16), xw, preferred_element_type=F32)

    y = jnp.concatenate(ys, axis=1) + d_ref[...] * xs
    y = y * z_ref[...]
    y = y * lax.rsqrt(jnp.mean(y * y, axis=-1, keepdims=True) + RMS_EPS)
    y_ref[...] = (y * ng_ref[...]).astype(BF16)


def _ssd(xbc, zs, dtc, dtr, a_col, a_row, d_tab, ng, tri_l, tri_u, d_inner, L):
    S = xbc.shape[0]
    G = SSM_GROUPS
    gw = d_inner // G
    hpg = gw // SSM_HEAD_DIM
    b_blk0 = d_inner // SSM_STATE
    c_blk0 = b_blk0 + G
    assert L == 2 * LANES and hpg % 2 == 0
    return pl.pallas_call(
        functools.partial(_ssd_body, L=L, hpg=hpg),
        grid=(S // L, G),
        in_specs=[
            pl.BlockSpec((L, gw), lambda c, g: (c, g)),
            pl.BlockSpec((L, SSM_STATE), lambda c, g: (c, b_blk0 + g)),
            pl.BlockSpec((L, SSM_STATE), lambda c, g: (c, c_blk0 + g)),
            pl.BlockSpec((L, gw), lambda c, g: (c, g)),
            pl.BlockSpec((None, L, LANES), lambda c, g: (g, c, 0)),
            pl.BlockSpec((None, hpg, L), lambda c, g: (g, 0, c)),
            pl.BlockSpec((None, 1, LANES), lambda c, g: (g, 0, 0)),
            pl.BlockSpec((None, hpg, LANES), lambda c, g: (g, 0, 0)),
            pl.BlockSpec((1, gw), lambda c, g: (0, g)),
            pl.BlockSpec((1, gw), lambda c, g: (0, g)),
            pl.BlockSpec((L, L), lambda c, g: (0, 0)),
            pl.BlockSpec((L, L), lambda c, g: (0, 0)),
        ],
        out_specs=pl.BlockSpec((L, gw), lambda c, g: (c, g)),
        out_shape=jax.ShapeDtypeStruct((S, d_inner), BF16),
        scratch_shapes=[pltpu.VMEM((G, SSM_STATE, gw), F32)],
        compiler_params=_params("arbitrary", "arbitrary"),
        name="ssd",
    )(xbc, xbc, xbc, zs, dtc, dtr, a_col, a_row, d_tab, ng, tri_l, tri_u)


def _rope_layout_index():
    x1 = jnp.arange(HALF_ROPE)
    x2 = x1 + HALF_ROPE
    return jnp.concatenate([x1, x2, x2, x1])


def _mla_layer(x, pos, norm_g, w_in, q_lora_g, w_uq, kv_lora_g, w_ukv, qn_g, kn_g, w_o, t):
    S, D = x.shape
    rl = _rope_layout_index()
    lora = Q_LORA_RANK + KV_LORA_RANK
    win_p = jnp.concatenate([w_in[:, :lora], w_in[:, lora + rl]], axis=1).astype(BF16)
    head_cols = jnp.concatenate([jnp.arange(QK_NOPE_DIM), QK_NOPE_DIM + rl])
    uq_cols = (jnp.arange(MLA_HEADS)[:, None] * QK_HEAD_DIM + head_cols[None, :]).reshape(-1)
    wuq_p = w_uq[:, uq_cols].astype(BF16)
    qgain = (qn_g[head_cols] * (QK_HEAD_DIM ** -0.5 * LOG2_E))[None, :]
    kgain = kn_g[head_cols][None, :]
    inv_freq = ROPE_THETA ** (-jnp.arange(0, QK_ROPE_DIM, 2, dtype=F32) / QK_ROPE_DIM)
    ones = jnp.ones((HALF_ROPE,), F32)
    zeros = jnp.zeros((2 * HALF_ROPE,), F32)
    rope_tab = jnp.zeros((SUBLANES, LANES), F32)
    rope_tab = rope_tab.at[0].set(jnp.tile(inv_freq, 4))
    rope_tab = rope_tab.at[1].set(jnp.concatenate([ones, ones, zeros]))
    rope_tab = rope_tab.at[2].set(jnp.concatenate([-ones, ones, zeros]))

    q, k, v = _mla_proj(x, pos, norm_g[None, :], win_p, q_lora_g[None, :], wuq_p,
                        kv_lora_g[None, :], w_ukv.astype(BF16), qgain, kgain, rope_tab, t["proj_tm"])
    bk = t["attn_bk"]
    vt = v.reshape(S // bk, bk, MLA_HEADS, V_HEAD_DIM).transpose(2, 0, 3, 1)
    o_t = _attention(q.T, k, vt, t["attn_bq"], bk)
    return _residual_matmul(x, o_t.T, w_o.astype(BF16), t["out_tm"])


def _ssm_layer(x, norm_g, w_in, conv_w, conv_b, dt_bias, a_log, d_skip, out_norm_g, w_out, t):
    S, D = x.shape
    n_heads = dt_bias.shape[0]
    d_inner = n_heads * SSM_HEAD_DIM
    G = SSM_GROUPS
    hpg = n_heads // G
    n_zx = w_in.shape[1] - n_heads
    L = SSM_CHUNK if S % SSM_CHUNK == 0 else S
    pad = LANES - n_heads
    wdt = jnp.pad(w_in[:, n_zx:], ((0, 0), (0, pad))).astype(BF16)
    dtb = jnp.pad(dt_bias, (0, pad))[None, :]
    zs, xbc, dt = _ssm_in(x, norm_g[None, :], w_in[:, :n_zx].astype(BF16), wdt, dtb,
                          conv_w, conv_b[None, :], d_inner, t["ssm_tm"], t["ssm_tn"])
    dt_g = dt[:, :n_heads].reshape(S, G, hpg).transpose(1, 0, 2)
    dtc = jnp.pad(dt_g, ((0, 0), (0, 0), (0, LANES - hpg)))
    dtr = dt_g.transpose(0, 2, 1)
    a = -jnp.exp(a_log).reshape(G, hpg)
    a_col = jnp.pad(a, ((0, 0), (0, LANES - hpg)))[:, None, :]
    a_row = jnp.broadcast_to(a[:, :, None], (G, hpg, LANES))
    d_tab = jnp.repeat(d_skip, SSM_HEAD_DIM)[None, :]
    tri_l = jnp.tril(jnp.ones((L, L), BF16))
    y = _ssd(xbc, zs, dtc, dtr, a_col, a_row, d_tab, out_norm_g[None, :], tri_l, tri_l.T, d_inner, L)
    return _residual_matmul(x, y, w_out.astype(BF16), t["out_tm"])


def kernel(x, positions, mix_norm_g, mlp_norm_g, mlp_w_in, mlp_w_out, mla_w_in, mla_q_norm_g, mla_w_uq, mla_kv_norm_g, mla_w_ukv, mla_qk_norm_q, mla_qk_norm_k, mla_w_o, ssm_w_in, ssm_conv_w, ssm_conv_b, ssm_dt_bias, ssm_a_log, ssm_d, ssm_norm_g, ssm_w_out):
    B, S, D = x.shape
    depth = mix_norm_g.shape[0]
    t = _tiles(S)
    outs = []
    for b in range(B):
        xb = x[b]
        pos = positions[b][:, None]
        for i in range(depth):
            j = i // 2
            if i % 2 == 0:
                xb = _mla_layer(xb, pos, mix_norm_g[i], mla_w_in[j], mla_q_norm_g[j], mla_w_uq[j],
                                mla_kv_norm_g[j], mla_w_ukv[j], mla_qk_norm_q[j], mla_qk_norm_k[j],
                                mla_w_o[j], t)
            else:
                xb = _ssm_layer(xb, mix_norm_g[i], ssm_w_in[j], ssm_conv_w[j], ssm_conv_b[j],
                                ssm_dt_bias[j], ssm_a_log[j], ssm_d[j], ssm_norm_g[j], ssm_w_out[j], t)
            xb = _mlp(xb, mlp_norm_g[i][None, :], mlp_w_in[i].astype(BF16), mlp_w_out[i].astype(BF16),
                      t["mlp_tm"], t["mlp_tf"])
        outs.append(xb)
    return jnp.stack(outs)
```

```python
import functools

import jax
import jax.numpy as jnp
from jax import lax
from jax.experimental import pallas as pl
from jax.experimental.pallas import tpu as pltpu

F32 = jnp.float32
BF16 = jnp.bfloat16

RMS_EPS = 1e-6

MLA_HEADS = 16
QK_NOPE_DIM = 128
QK_ROPE_DIM = 64
QK_HEAD_DIM = QK_NOPE_DIM + QK_ROPE_DIM
V_HEAD_DIM = 128
Q_LORA_RANK = 512
KV_LORA_RANK = 512
ROPE_THETA = 10000.0
SSM_HEAD_DIM = 64
SSM_GROUPS = 8
SSM_STATE = 128
SSM_CONV_WIDTH = 4
SSM_CHUNK = 256

LANES = 128
SUBLANES = 8
VMEM_LIMIT_BYTES = 56 * 1024 * 1024

QK_PAD_DIM = 2 * LANES
HALF_ROPE = QK_ROPE_DIM // 2

MASK_VALUE = -1e30
LOG2_E = 1.4426950408889634
ATTN_BOUND_LIMIT = 60.0
ATTN_BOUND_SLACK = 1.001


def _tiles(S):
    return dict(
        mlp_tm=min(1024, S), mlp_tf=512,
        proj_tm=min(256, S),
        attn_bq=min(1024, S), attn_bk=min(512, S),
        ssm_tm=min(1024, S), ssm_tn=1024,
        out_tm=min(512, S),
    )


def _params(*sem):
    return pltpu.CompilerParams(dimension_semantics=sem, vmem_limit_bytes=VMEM_LIMIT_BYTES)


def _rms_rows(x, g):
    ms = jnp.mean(x * x, axis=-1, keepdims=True)
    return x * lax.rsqrt(ms + RMS_EPS) * g


def _resident(shape):
    nd = len(shape)
    return pl.BlockSpec(shape, lambda *_: (0,) * nd, pipeline_mode=pl.Buffered(1))


def _mlp_body(x_ref, g_ref, w1_ref, w2_ref, o_ref, h_ref):
    @pl.when(pl.program_id(1) == 0)
    def _():
        x = x_ref[...]
        h_ref[...] = _rms_rows(x, g_ref[...]).astype(BF16)
        o_ref[...] = x

    u = jnp.dot(h_ref[...], w1_ref[...], preferred_element_type=F32)
    u = jnp.maximum(u, 0.0)
    u = (u * u).astype(BF16)
    o_ref[...] += jnp.dot(u, w2_ref[...], preferred_element_type=F32)


def _mlp(x, g, w1, w2, tm, tf):
    S, D = x.shape
    F = w1.shape[1]
    return pl.pallas_call(
        _mlp_body,
        grid=(S // tm, F // tf),
        in_specs=[
            pl.BlockSpec((tm, D), lambda i, f: (i, 0)),
            pl.BlockSpec((1, D), lambda i, f: (0, 0)),
            pl.BlockSpec((D, tf), lambda i, f: (0, f)),
            pl.BlockSpec((tf, D), lambda i, f: (f, 0)),
        ],
        out_specs=pl.BlockSpec((tm, D), lambda i, f: (i, 0)),
        out_shape=jax.ShapeDtypeStruct((S, D), F32),
        scratch_shapes=[pltpu.VMEM((tm, D), BF16)],
        compiler_params=_params("parallel", "arbitrary"),
        name="mlp",
    )(x, g, w1, w2)


def _resmm_body(x_ref, a_ref, w_ref, o_ref):
    o_ref[...] = x_ref[...] + jnp.dot(a_ref[...], w_ref[...], preferred_element_type=F32)


def _residual_matmul(x, a, w, tm):
    S, D = x.shape
    K = a.shape[1]
    return pl.pallas_call(
        _resmm_body,
        grid=(S // tm,),
        in_specs=[
            pl.BlockSpec((tm, D), lambda i: (i, 0)),
            pl.BlockSpec((tm, K), lambda i: (i, 0)),
            _resident((K, D)),
        ],
        out_specs=pl.BlockSpec((tm, D), lambda i: (i, 0)),
        out_shape=jax.ShapeDtypeStruct((S, D), F32),
        compiler_params=_params("parallel"),
        name="residual_matmul",
    )(x, a, w)


def _mla_proj_body(x_ref, pos_ref, g_ref, win_ref, qlg_ref, wuq_ref, kvlg_ref, wukv_ref,
                   qgain_ref, kgain_ref, rope_ref, q_ref, k_ref, v_ref):
    h = _rms_rows(x_ref[...], g_ref[...]).astype(BF16)
    a = jnp.dot(h, win_ref[...], preferred_element_type=F32)
    cq = _rms_rows(a[:, :Q_LORA_RANK], qlg_ref[...]).astype(BF16)
    ckv = _rms_rows(a[:, Q_LORA_RANK:Q_LORA_RANK + KV_LORA_RANK], kvlg_ref[...]).astype(BF16)
    kr = a[:, Q_LORA_RANK + KV_LORA_RANK:]
    q = jnp.dot(cq, wuq_ref[...], preferred_element_type=F32)
    kv = jnp.dot(ckv, wukv_ref[...], preferred_element_type=F32)

    ang = pos_ref[...].astype(F32) * rope_ref[0:1, :]
    cos_t = jnp.cos(ang) * rope_ref[1:2, :]
    sin_t = jnp.sin(ang) * rope_ref[2:3, :]

    def rope(v):
        return v * cos_t + pltpu.roll(v, 2 * HALF_ROPE, 1) * sin_t

    qgain = qgain_ref[...]
    kgain = kgain_ref[...]
    inv_d = 1.0 / QK_HEAD_DIM
    kr_ss = 0.5 * jnp.sum(kr * kr, axis=-1, keepdims=True)
    kr_rot = rope(kr * kgain[:, LANES:])
    for hd in range(MLA_HEADS):
        lo = hd * QK_PAD_DIM
        mid = lo + LANES
        hi = lo + QK_PAD_DIM
        qn = q[:, lo:mid]
        qr = q[:, mid:hi]
        q_ss = jnp.sum(qn * qn, axis=-1, keepdims=True) + 0.5 * jnp.sum(qr * qr, axis=-1, keepdims=True)
        q_r = lax.rsqrt(q_ss * inv_d + RMS_EPS)
        q_ref[:, lo:mid] = (qn * q_r * qgain[:, :LANES]).astype(BF16)
        q_ref[:, mid:hi] = rope(qr * q_r * qgain[:, LANES:]).astype(BF16)
        kn = kv[:, lo:mid]
        k_ss = jnp.sum(kn * kn, axis=-1, keepdims=True) + kr_ss
        k_r = lax.rsqrt(k_ss * inv_d + RMS_EPS)
        k_ref[:, lo:mid] = (kn * k_r * kgain[:, :LANES]).astype(BF16)
        k_ref[:, mid:hi] = (kr_rot * k_r).astype(BF16)
        v_ref[:, hd * V_HEAD_DIM:(hd + 1) * V_HEAD_DIM] = kv[:, mid:hi].astype(BF16)


def _mla_proj(x, pos, g, win, qlg, wuq, kvlg, wukv, qgain, kgain, rope_tab, tm):
    S, D = x.shape
    HQ = MLA_HEADS * QK_PAD_DIM
    HV = MLA_HEADS * V_HEAD_DIM
    row = lambda w: pl.BlockSpec((tm, w), lambda i: (i, 0))
    return pl.pallas_call(
        _mla_proj_body,
        grid=(S // tm,),
        in_specs=[
            row(D), row(1), _resident(g.shape), _resident(win.shape), _resident(qlg.shape),
            _resident(wuq.shape), _resident(kvlg.shape), _resident(wukv.shape),
            _resident(qgain.shape), _resident(kgain.shape), _resident(rope_tab.shape),
        ],
        out_specs=[row(HQ), row(HQ), row(HV)],
        out_shape=[
            jax.ShapeDtypeStruct((S, HQ), BF16),
            jax.ShapeDtypeStruct((S, HQ), BF16),
            jax.ShapeDtypeStruct((S, HV), BF16),
        ],
        compiler_params=_params("parallel"),
        name="mla_proj",
    )(x, pos, g, win, qlg, wuq, kvlg, wukv, qgain, kgain, rope_tab)


def _attn_exact_path(qi, qt, k_ref, vt_ref, m_ref, l_ref, acc_ref, s_bufs, *, bq, bk):
    m_ref[...] = jnp.full(m_ref.shape, MASK_VALUE, F32)

    def scores(kb, s_ref, lo):
        kblk = k_ref[pl.ds(pl.multiple_of(kb * bk, bk), bk), :]
        s_ref[:, lo:] = jnp.dot(kblk, qt[:, lo:], preferred_element_type=F32)

    def update(kb, s_ref, lo, diagonal):
        s = s_ref[:, lo:]
        if diagonal:
            key = lax.broadcasted_iota(jnp.int32, s.shape, 0)
            qry = lax.broadcasted_iota(jnp.int32, s.shape, 1)
            s = jnp.where(key <= qry, s, MASK_VALUE)
        m_old = m_ref[:, lo:]
        m_new = jnp.maximum(m_old, jnp.max(s, axis=0, keepdims=True))
        p = jnp.exp2(s - m_new)
        alpha = jnp.exp2(m_old - m_new)
        l_ref[:, lo:] = alpha * l_ref[:, lo:] + jnp.sum(p, axis=0, keepdims=True)
        pv = jnp.dot(vt_ref[kb], p.astype(BF16), preferred_element_type=F32)
        acc_ref[:, lo:] = alpha * acc_ref[:, lo:] + pv
        m_ref[:, lo:] = m_new

    per_q = bq // bk
    n_full = qi * per_q
    scores(0, s_bufs[0], 0)

    def full_blocks(t, carry):
        base = t * per_q
        for d in range(per_q):
            scores(base + d + 1, s_bufs[(d + 1) % 2], 0)
            update(base + d, s_bufs[d % 2], 0, False)
        return carry

    lax.fori_loop(0, qi, full_blocks, 0)
    for d in range(per_q):
        if d + 1 < per_q:
            scores(n_full + d + 1, s_bufs[(d + 1) % 2], (d + 1) * bk)
        update(n_full + d, s_bufs[d % 2], d * bk, True)


def _attn_bounded_path(qi, qt, bound, k_ref, vt_ref, l_ref, acc_ref, *, bq, bk):
    per_q = bq // bk

    def block(kb, lo, diagonal):
        kblk = k_ref[pl.ds(pl.multiple_of(kb * bk, bk), bk), :]
        s = jnp.dot(kblk, qt[:, lo:], preferred_element_type=F32)
        p = jnp.exp2(s - bound[:, lo:])
        if diagonal:
            key = lax.broadcasted_iota(jnp.int32, s.shape, 0)
            qry = lax.broadcasted_iota(jnp.int32, s.shape, 1)
            p = jnp.where(key <= qry, p, 0.0)
        l_ref[:, lo:] += jnp.sum(p, axis=0, keepdims=True)
        acc_ref[:, lo:] += jnp.dot(vt_ref[kb], p.astype(BF16), preferred_element_type=F32)

    def full_blocks(t, carry):
        for d in range(per_q):
            block(t * per_q + d, 0, False)
        return carry

    lax.fori_loop(0, qi, full_blocks, 0)
    for d in range(per_q):
        block(qi * per_q + d, d * bk, True)


def _attn_body(qt_ref, k_ref, vt_ref, o_ref, m_ref, l_ref, acc_ref, s0_ref, s1_ref, kmax_ref, *, bq, bk):
    qi = pl.program_id(1)

    @pl.when(qi == 0)
    def _():
        def chunk(c, best):
            kc = k_ref[pl.ds(pl.multiple_of(c * bk, bk), bk), :].astype(F32)
            return jnp.maximum(best, jnp.max(jnp.sum(kc * kc, axis=1, keepdims=True), axis=0, keepdims=True))

        best = lax.fori_loop(0, k_ref.shape[0] // bk, chunk, jnp.zeros((1, 1), F32))
        kmax_ref[0] = best[0, 0]

    qt = qt_ref[...]
    l_ref[...] = jnp.zeros(l_ref.shape, F32)
    acc_ref[...] = jnp.zeros(acc_ref.shape, F32)
    qf = qt.astype(F32)
    q_norm2 = jnp.sum(qf * qf, axis=0, keepdims=True)
    bound = jnp.sqrt(q_norm2 * kmax_ref[0]) * ATTN_BOUND_SLACK + ATTN_BOUND_SLACK
    bounded = jnp.max(bound) <= ATTN_BOUND_LIMIT

    @pl.when(bounded)
    def _():
        _attn_bounded_path(qi, qt, bound, k_ref, vt_ref, l_ref, acc_ref, bq=bq, bk=bk)

    @pl.when(jnp.logical_not(bounded))
    def _():
        _attn_exact_path(qi, qt, k_ref, vt_ref, m_ref, l_ref, acc_ref, (s0_ref, s1_ref), bq=bq, bk=bk)

    o_ref[...] = (acc_ref[...] / l_ref[...]).T.astype(o_ref.dtype)


def _attention(qt, k, vt, bq, bk):
    S = k.shape[0]
    H = MLA_HEADS
    nkb = S // bk
    assert bq % (2 * bk) == 0 and S % bq == 0
    return pl.pallas_call(
        functools.partial(_attn_body, bq=bq, bk=bk),
        grid=(H, S // bq),
        in_specs=[
            pl.BlockSpec((QK_PAD_DIM, bq), lambda h, i: (h, i)),
            pl.BlockSpec((S, QK_PAD_DIM), lambda h, i: (0, h)),
            pl.BlockSpec((None, nkb, V_HEAD_DIM, bk), lambda h, i: (h, 0, 0, 0)),
        ],
        out_specs=pl.BlockSpec((bq, V_HEAD_DIM), lambda h, i: (i, h)),
        out_shape=jax.ShapeDtypeStruct((S, H * V_HEAD_DIM), BF16),
        scratch_shapes=[
            pltpu.VMEM((1, bq), F32),
            pltpu.VMEM((1, bq), F32),
            pltpu.VMEM((V_HEAD_DIM, bq), F32),
            pltpu.VMEM((bk, bq), F32),
            pltpu.VMEM((bk, bq), F32),
            pltpu.SMEM((1,), F32),
        ],
        compiler_params=_params("arbitrary", "arbitrary"),
        name="attention",
    )(qt, k, vt)


def _silu(x):
    return x * jax.nn.sigmoid(x)


SSM_SUB_TILE = 2 * LANES


def _ssm_z_body(x_ref, g_ref, w_ref, wdt_ref, dtb_ref, z_ref, dt_ref, h_ref):
    @pl.when(pl.program_id(1) == 0)
    def _():
        h = _rms_rows(x_ref[...], g_ref[...]).astype(BF16)
        h_ref[...] = h
        dt_raw = jnp.dot(h, wdt_ref[...], preferred_element_type=F32)
        dt_ref[...] = jax.nn.softplus(dt_raw + dtb_ref[...])

    h = h_ref[...]
    for c0 in range(0, z_ref.shape[1], SSM_SUB_TILE):
        cs = slice(c0, c0 + SSM_SUB_TILE)
        z_ref[:, cs] = _silu(jnp.dot(h, w_ref[:, cs], preferred_element_type=F32))


def _ssm_xbc_body(x_ref, g_ref, w_ref, cw_ref, cb_ref, o_ref, h_ref, carry_ref):
    i = pl.program_id(0)
    n = pl.program_id(1)
    tm = o_ref.shape[0]

    @pl.when(n == 0)
    def _():
        h_ref[...] = _rms_rows(x_ref[...], g_ref[...]).astype(BF16)

    @pl.when(i == 0)
    def _():
        carry_ref[n] = jnp.zeros(carry_ref.shape[1:], F32)

    h = h_ref[...]
    for c0 in range(0, o_ref.shape[1], SSM_SUB_TILE):
        cs = slice(c0, c0 + SSM_SUB_TILE)
        pre = jnp.dot(h, w_ref[:, cs], preferred_element_type=F32)
        full = jnp.concatenate([carry_ref[n, :, cs], pre], axis=0)
        carry_ref[n, :, cs] = pre[tm - SUBLANES:, :]
        cw = cw_ref[:, cs]
        acc = cb_ref[:, cs] + cw[SSM_CONV_WIDTH - 1:SSM_CONV_WIDTH, :] * pre
        for kk in range(SSM_CONV_WIDTH - 1):
            shift = SSM_CONV_WIDTH - 1 - kk
            acc = acc + cw[kk:kk + 1, :] * pltpu.roll(full, shift, 0)[SUBLANES:, :]
        o_ref[:, cs] = _silu(acc)


def _ssm_in(x, g, w, wdt, dtb, cw, cb, d_inner, tm, tn):
    S, D = x.shape
    N = w.shape[1]
    nz = d_inner // tn
    nx = (N - d_inner) // tn
    x_spec = pl.BlockSpec((tm, D), lambda i, n: (i, 0))
    g_spec = pl.BlockSpec((1, D), lambda i, n: (0, 0))
    zs, dt = pl.pallas_call(
        _ssm_z_body,
        grid=(S // tm, nz),
        in_specs=[
            x_spec, g_spec,
            pl.BlockSpec((D, tn), lambda i, n: (0, n)),
            pl.BlockSpec((D, LANES), lambda i, n: (0, 0)),
            pl.BlockSpec((1, LANES), lambda i, n: (0, 0)),
        ],
        out_specs=[
            pl.BlockSpec((tm, tn), lambda i, n: (i, n)),
            pl.BlockSpec((tm, LANES), lambda i, n: (i, 0)),
        ],
        out_shape=[
            jax.ShapeDtypeStruct((S, d_inner), F32),
            jax.ShapeDtypeStruct((S, LANES), F32),
        ],
        scratch_shapes=[pltpu.VMEM((tm, D), BF16)],
        compiler_params=_params("parallel", "arbitrary"),
        name="ssm_in_z",
    )(x, g, w, wdt, dtb)
    xbc = pl.pallas_call(
        _ssm_xbc_body,
        grid=(S // tm, nx),
        in_specs=[
            x_spec, g_spec,
            pl.BlockSpec((D, tn), lambda i, n: (0, nz + n)),
            pl.BlockSpec((SSM_CONV_WIDTH, tn), lambda i, n: (0, n)),
            pl.BlockSpec((1, tn), lambda i, n: (0, n)),
        ],
        out_specs=pl.BlockSpec((tm, tn), lambda i, n: (i, n)),
        out_shape=jax.ShapeDtypeStruct((S, N - d_inner), F32),
        scratch_shapes=[
            pltpu.VMEM((tm, D), BF16),
            pltpu.VMEM((nx, SUBLANES, tn), F32),
        ],
        compiler_params=_params("arbitrary", "arbitrary"),
        name="ssm_in_xbc",
    )(x, g, w, cw, cb)
    return zs, xbc, dt


def _split3(a):
    hi = a.astype(BF16)
    r1 = a - hi.astype(F32)
    mid = r1.astype(BF16)
    lo = (r1 - mid.astype(F32)).astype(BF16)
    return hi, mid, lo


def _ssd_body(xs_ref, b_ref, c_ref, z_ref, dtc_ref, dtr_ref, ac_ref, ar_ref, d_ref, ng_ref,
              tl_ref, tu_ref, y_ref, st_ref, *, L, hpg):
    c = pl.program_id(0)
    g = pl.program_id(1)

    @pl.when(c == 0)
    def _():
        st_ref[g] = jnp.zeros(st_ref.shape[1:], F32)

    xs = xs_ref[...]
    bmat = b_ref[...]
    cmat = c_ref[...].astype(BF16)
    dt = dtc_ref[...]
    la = dt * ac_ref[...]
    la_t = dtr_ref[...] * ar_ref[:, 0:1]
    cum = sum(jnp.dot(tl_ref[...], p, preferred_element_type=F32) for p in _split3(la))
    cum_t = sum(jnp.dot(p, tu_ref[...], preferred_element_type=F32) for p in _split3(la_t))

    cb = lax.dot_general(cmat, bmat.astype(BF16), (((1,), (1,)), ((), ())),
                         preferred_element_type=F32)
    half = LANES
    t_out = lax.broadcasted_iota(jnp.int32, (half, half), 0)
    t_in = lax.broadcasted_iota(jnp.int32, (half, half), 1)
    causal = t_in <= t_out
    first_head = lax.broadcasted_iota(jnp.int32, (L, LANES), 1) < SSM_HEAD_DIM
    zero_q = jnp.zeros((half, half), F32)

    state = st_ref[g]
    y_off = jnp.dot(cmat, state.astype(BF16), preferred_element_type=F32)
    cum_last = cum[L - 1:L, :]

    def col(a, j, rows):
        return jnp.broadcast_to(a[:, j:j + 1], (rows, LANES))

    ys, xws, cds = [], [], []
    for pr in range(hpg // 2):
        j0, j1 = 2 * pr, 2 * pr + 1
        sl = slice(pr * LANES, (pr + 1) * LANES)
        cum_b = (col(cum, j0, L), col(cum, j1, L))
        cum_p = jnp.where(first_head, cum_b[0], cum_b[1])
        dt_p = jnp.where(first_head, col(dt, j0, L), col(dt, j1, L))
        last_p = jnp.where(first_head[:1], col(cum_last, j0, 1), col(cum_last, j1, 1))
        xdt = xs[:, sl] * dt_p
        ms = []
        for cb_j, j in zip(cum_b, (j0, j1)):
            row = cum_t[j:j + 1, :]
            d_tl = jnp.exp(jnp.where(causal, cb_j[:half] - row[:, :half], -jnp.inf))
            d_bl = jnp.exp(cb_j[half:] - row[:, :half])
            d_br = jnp.exp(jnp.where(causal, cb_j[half:] - row[:, half:], -jnp.inf))
            top = jnp.concatenate([cb[:half, :half] * d_tl, zero_q], axis=1)
            bot = jnp.concatenate([cb[half:, :half] * d_bl, cb[half:, half:] * d_br], axis=1)
            ms.append(jnp.concatenate([top, bot], axis=0).astype(BF16))
        lhs = jnp.concatenate(ms, axis=1)
        rhs = jnp.concatenate([jnp.where(first_head, xdt, 0.0).astype(BF16),
                               jnp.where(first_head, 0.0, xdt).astype(BF16)], axis=0)
        y_diag = jnp.dot(lhs, rhs, preferred_element_type=F32)
        ys.append(y_diag + y_off[:, sl] * jnp.exp(cum_p))
        xws.append((xdt * jnp.exp(last_p - cum_p)).astype(BF16))
        cds.append(jnp.exp(last_p))

    xw = jnp.concatenate(xws, axis=1)
    chunk_decay = jnp.concatenate(cds, axis=1)
    st_ref[g] = state * chunk_decay + jnp.dot(bmat.T.astype(BF16), xw, preferred_element_type=F32)

    y = jnp.concatenate(ys, axis=1) + d_ref[...] * xs
    y = y * z_ref[...]
    y = y * lax.rsqrt(jnp.mean(y * y, axis=-1, keepdims=True) + RMS_EPS)
    y_ref[...] = (y * ng_ref[...]).astype(BF16)


def _ssd(xbc, zs, dtc, dtr, a_col, a_row, d_tab, ng, tri_l, tri_u, d_inner, L):
    S = xbc.shape[0]
    G = SSM_GROUPS
    gw = d_inner // G
    hpg = gw // SSM_HEAD_DIM
    b_blk0 = d_inner // SSM_STATE
    c_blk0 = b_blk0 + G
    assert L == 2 * LANES and hpg % 2 == 0
    return pl.pallas_call(
        functools.partial(_ssd_body, L=L, hpg=hpg),
        grid=(S // L, G),
        in_specs=[
            pl.BlockSpec((L, gw), lambda c, g: (c, g)),
            pl.BlockSpec((L, SSM_STATE), lambda c, g: (c, b_blk0 + g)),
            pl.BlockSpec((L, SSM_STATE), lambda c, g: (c, c_blk0 + g)),
            pl.BlockSpec((L, gw), lambda c, g: (c, g)),
            pl.BlockSpec((None, L, LANES), lambda c, g: (g, c, 0)),
            pl.BlockSpec((None, hpg, L), lambda c, g: (g, 0, c)),
            pl.BlockSpec((None, 1, LANES), lambda c, g: (g, 0, 0)),
            pl.BlockSpec((None, hpg, LANES), lambda c, g: (g, 0, 0)),
            pl.BlockSpec((1, gw), lambda c, g: (0, g)),
            pl.BlockSpec((1, gw), lambda c, g: (0, g)),
            pl.BlockSpec((L, L), lambda c, g: (0, 0)),
            pl.BlockSpec((L, L), lambda c, g: (0, 0)),
        ],
        out_specs=pl.BlockSpec((L, gw), lambda c, g: (c, g)),
        out_shape=jax.ShapeDtypeStruct((S, d_inner), BF16),
        scratch_shapes=[pltpu.VMEM((G, SSM_STATE, gw), F32)],
        compiler_params=_params("arbitrary", "arbitrary"),
        name="ssd",
    )(xbc, xbc, xbc, zs, dtc, dtr, a_col, a_row, d_tab, ng, tri_l, tri_u)


def _rope_layout_index():
    x1 = jnp.arange(HALF_ROPE)
    x2 = x1 + HALF_ROPE
    return jnp.concatenate([x1, x2, x2, x1])


def _mla_layer(x, pos, norm_g, w_in, q_lora_g, w_uq, kv_lora_g, w_ukv, qn_g, kn_g, w_o, t):
    S, D = x.shape
    rl = _rope_layout_index()
    lora = Q_LORA_RANK + KV_LORA_RANK
    win_p = jnp.concatenate([w_in[:, :lora], w_in[:, lora + rl]], axis=1).astype(BF16)
    head_cols = jnp.concatenate([jnp.arange(QK_NOPE_DIM), QK_NOPE_DIM + rl])
    uq_cols = (jnp.arange(MLA_HEADS)[:, None] * QK_HEAD_DIM + head_cols[None, :]).reshape(-1)
    wuq_p = w_uq[:, uq_cols].astype(BF16)
    qgain = (qn_g[head_cols] * (QK_HEAD_DIM ** -0.5 * LOG2_E))[None, :]
    kgain = kn_g[head_cols][None, :]
    inv_freq = ROPE_THETA ** (-jnp.arange(0, QK_ROPE_DIM, 2, dtype=F32) / QK_ROPE_DIM)
    ones = jnp.ones((HALF_ROPE,), F32)
    zeros = jnp.zeros((2 * HALF_ROPE,), F32)
    rope_tab = jnp.zeros((SUBLANES, LANES), F32)
    rope_tab = rope_tab.at[0].set(jnp.tile(inv_freq, 4))
    rope_tab = rope_tab.at[1].set(jnp.concatenate([ones, ones, zeros]))
    rope_tab = rope_tab.at[2].set(jnp.concatenate([-ones, ones, zeros]))

    q, k, v = _mla_proj(x, pos, norm_g[None, :], win_p, q_lora_g[None, :], wuq_p,
                        kv_lora_g[None, :], w_ukv.astype(BF16), qgain, kgain, rope_tab, t["proj_tm"])
    bk = t["attn_bk"]
    vt = v.reshape(S // bk, bk, MLA_HEADS, V_HEAD_DIM).transpose(2, 0, 3, 1)
    o = _attention(q.T, k, vt, t["attn_bq"], bk)
    return _residual_matmul(x, o, w_o.astype(BF16), t["out_tm"])


def _ssm_layer(x, norm_g, w_in, conv_w, conv_b, dt_bias, a_log, d_skip, out_norm_g, w_out, t):
    S, D = x.shape
    n_heads = dt_bias.shape[0]
    d_inner = n_heads * SSM_HEAD_DIM
    G = SSM_GROUPS
    hpg = n_heads // G
    n_zx = w_in.shape[1] - n_heads
    L = SSM_CHUNK if S % SSM_CHUNK == 0 else S
    pad = LANES - n_heads
    wdt = jnp.pad(w_in[:, n_zx:], ((0, 0), (0, pad))).astype(BF16)
    dtb = jnp.pad(dt_bias, (0, pad))[None, :]
    zs, xbc, dt = _ssm_in(x, norm_g[None, :], w_in[:, :n_zx].astype(BF16), wdt, dtb,
                          conv_w, conv_b[None, :], d_inner, t["ssm_tm"], t["ssm_tn"])
    dt_g = dt[:, :n_heads].reshape(S, G, hpg).transpose(1, 0, 2)
    dtc = jnp.pad(dt_g, ((0, 0), (0, 0), (0, LANES - hpg)))
    dtr = dt_g.transpose(0, 2, 1)
    a = -jnp.exp(a_log).reshape(G, hpg)
    a_col = jnp.pad(a, ((0, 0), (0, LANES - hpg)))[:, None, :]
    a_row = jnp.broadcast_to(a[:, :, None], (G, hpg, LANES))
    d_tab = jnp.repeat(d_skip, SSM_HEAD_DIM)[None, :]
    tri_l = jnp.tril(jnp.ones((L, L), BF16))
    y = _ssd(xbc, zs, dtc, dtr, a_col, a_row, d_tab, out_norm_g[None, :], tri_l, tri_l.T, d_inner, L)
    return _residual_matmul(x, y, w_out.astype(BF16), t["out_tm"])


def kernel(x, positions, mix_norm_g, mlp_norm_g, mlp_w_in, mlp_w_out, mla_w_in, mla_q_norm_g, mla_w_uq, mla_kv_norm_g, mla_w_ukv, mla_qk_norm_q, mla_qk_norm_k, mla_w_o, ssm_w_in, ssm_conv_w, ssm_conv_b, ssm_dt_bias, ssm_a_log, ssm_d, ssm_norm_g, ssm_w_out):
    B, S, D = x.shape
    depth = mix_norm_g.shape[0]
    t = _tiles(S)
    outs = []
    for b in range(B):
        xb = x[b]
        pos = positions[b][:, None]
        for i in range(depth):
            j = i // 2
            if i % 2 == 0:
                xb = _mla_layer(xb, pos, mix_norm_g[i], mla_w_in[j], mla_q_norm_g[j], mla_w_uq[j],
                                mla_kv_norm_g[j], mla_w_ukv[j], mla_qk_norm_q[j], mla_qk_norm_k[j],
                                mla_w_o[j], t)
            else:
                xb = _ssm_layer(xb, mix_norm_g[i], ssm_w_in[j], ssm_conv_w[j], ssm_conv_b[j],
                                ssm_dt_bias[j], ssm_a_log[j], ssm_d[j], ssm_norm_g[j], ssm_w_out[j], t)
            xb = _mlp(xb, mlp_norm_g[i][None, :], mlp_w_in[i].astype(BF16), mlp_w_out[i].astype(BF16),
                      t["mlp_tm"], t["mlp_tf"])
        outs.append(xb)
    return jnp.stack(outs)
```

```python
import functools

import jax
import jax.numpy as jnp
from jax import lax
from jax.experimental import pallas as pl
from jax.experimental.pallas import tpu as pltpu

F32 = jnp.float32
BF16 = jnp.bfloat16

RMS_EPS = 1e-6

MLA_HEADS = 16
QK_NOPE_DIM = 128
QK_ROPE_DIM = 64
QK_HEAD_DIM = QK_NOPE_DIM + QK_ROPE_DIM
V_HEAD_DIM = 128
Q_LORA_RANK = 512
KV_LORA_RANK = 512
ROPE_THETA = 10000.0
SSM_HEAD_DIM = 64
SSM_GROUPS = 8
SSM_STATE = 128
SSM_CONV_WIDTH = 4
SSM_CHUNK = 256

LANES = 128
SUBLANES = 8
VMEM_LIMIT_BYTES = 56 * 1024 * 1024

QK_PAD_DIM = 2 * LANES
HALF_ROPE = QK_ROPE_DIM // 2

MASK_VALUE = -1e30
LOG2_E = 1.4426950408889634
ATTN_BOUND_LIMIT = 60.0
ATTN_BOUND_SLACK = 1.001


def _tiles(S):
    return dict(
        mlp_tm=min(1024, S), mlp_tf=512,
        proj_tm=min(256, S),
        attn_bq=min(1024, S), attn_bk=min(512, S),
        ssm_tm=min(1024, S), ssm_tn=1024,
        out_tm=min(512, S),
        ssd_groups=2,
    )


def _params(*sem):
    return pltpu.CompilerParams(dimension_semantics=sem, vmem_limit_bytes=VMEM_LIMIT_BYTES)


def _rms_rows(x, g):
    ms = jnp.mean(x * x, axis=-1, keepdims=True)
    return x * lax.rsqrt(ms + RMS_EPS) * g


def _resident(shape):
    nd = len(shape)
    return pl.BlockSpec(shape, lambda *_: (0,) * nd, pipeline_mode=pl.Buffered(1))


def _mlp_body(x_ref, g_ref, w1_ref, w2_ref, o_ref, h_ref):
    @pl.when(pl.program_id(1) == 0)
    def _():
        x = x_ref[...]
        h_ref[...] = _rms_rows(x, g_ref[...]).astype(BF16)
        o_ref[...] = x

    u = jnp.dot(h_ref[...], w1_ref[...], preferred_element_type=F32)
    u = jnp.maximum(u, 0.0)
    u = (u * u).astype(BF16)
    o_ref[...] += jnp.dot(u, w2_ref[...], preferred_element_type=F32)


def _mlp(x, g, w1, w2, tm, tf):
    S, D = x.shape
    F = w1.shape[1]
    return pl.pallas_call(
        _mlp_body,
        grid=(S // tm, F // tf),
        in_specs=[
            pl.BlockSpec((tm, D), lambda i, f: (i, 0)),
            pl.BlockSpec((1, D), lambda i, f: (0, 0)),
            pl.BlockSpec((D, tf), lambda i, f: (0, f)),
            pl.BlockSpec((tf, D), lambda i, f: (f, 0)),
        ],
        out_specs=pl.BlockSpec((tm, D), lambda i, f: (i, 0)),
        out_shape=jax.ShapeDtypeStruct((S, D), F32),
        scratch_shapes=[pltpu.VMEM((tm, D), BF16)],
        compiler_params=_params("parallel", "arbitrary"),
        name="mlp",
    )(x, g, w1, w2)


def _resmm_body(x_ref, a_ref, w_ref, o_ref):
    o_ref[...] = x_ref[...] + jnp.dot(a_ref[...], w_ref[...], preferred_element_type=F32)


def _residual_matmul(x, a, w, tm):
    S, D = x.shape
    K = a.shape[1]
    return pl.pallas_call(
        _resmm_body,
        grid=(S // tm,),
        in_specs=[
            pl.BlockSpec((tm, D), lambda i: (i, 0)),
            pl.BlockSpec((tm, K), lambda i: (i, 0)),
            _resident((K, D)),
        ],
        out_specs=pl.BlockSpec((tm, D), lambda i: (i, 0)),
        out_shape=jax.ShapeDtypeStruct((S, D), F32),
        compiler_params=_params("parallel"),
        name="residual_matmul",
    )(x, a, w)


def _mla_proj_body(x_ref, pos_ref, g_ref, win_ref, qlg_ref, wuq_ref, kvlg_ref, wukv_ref,
                   qgain_ref, kgain_ref, rope_ref, q_ref, k_ref, v_ref):
    h = _rms_rows(x_ref[...], g_ref[...]).astype(BF16)
    a = jnp.dot(h, win_ref[...], preferred_element_type=F32)
    cq = _rms_rows(a[:, :Q_LORA_RANK], qlg_ref[...]).astype(BF16)
    ckv = _rms_rows(a[:, Q_LORA_RANK:Q_LORA_RANK + KV_LORA_RANK], kvlg_ref[...]).astype(BF16)
    kr = a[:, Q_LORA_RANK + KV_LORA_RANK:]
    q = jnp.dot(cq, wuq_ref[...], preferred_element_type=F32)
    kv = jnp.dot(ckv, wukv_ref[...], preferred_element_type=F32)

    ang = pos_ref[...].astype(F32) * rope_ref[0:1, :]
    cos_t = jnp.cos(ang) * rope_ref[1:2, :]
    sin_t = jnp.sin(ang) * rope_ref[2:3, :]

    def rope(v):
        return v * cos_t + pltpu.roll(v, 2 * HALF_ROPE, 1) * sin_t

    qgain = qgain_ref[...]
    kgain = kgain_ref[...]
    inv_d = 1.0 / QK_HEAD_DIM
    kr_ss = 0.5 * jnp.sum(kr * kr, axis=-1, keepdims=True)
    kr_rot = rope(kr * kgain[:, LANES:])
    for hd in range(MLA_HEADS):
        lo = hd * QK_PAD_DIM
        mid = lo + LANES
        hi = lo + QK_PAD_DIM
        qn = q[:, lo:mid]
        qr = q[:, mid:hi]
        q_ss = jnp.sum(qn * qn, axis=-1, keepdims=True) + 0.5 * jnp.sum(qr * qr, axis=-1, keepdims=True)
        q_r = lax.rsqrt(q_ss * inv_d + RMS_EPS)
        q_ref[:, lo:mid] = (qn * q_r * qgain[:, :LANES]).astype(BF16)
        q_ref[:, mid:hi] = rope(qr * q_r * qgain[:, LANES:]).astype(BF16)
        kn = kv[:, lo:mid]
        k_ss = jnp.sum(kn * kn, axis=-1, keepdims=True) + kr_ss
        k_r = lax.rsqrt(k_ss * inv_d + RMS_EPS)
        k_ref[:, lo:mid] = (kn * k_r * kgain[:, :LANES]).astype(BF16)
        k_ref[:, mid:hi] = (kr_rot * k_r).astype(BF16)
        v_ref[:, hd * V_HEAD_DIM:(hd + 1) * V_HEAD_DIM] = kv[:, mid:hi].astype(BF16)


def _mla_proj(x, pos, g, win, qlg, wuq, kvlg, wukv, qgain, kgain, rope_tab, tm):
    S, D = x.shape
    HQ = MLA_HEADS * QK_PAD_DIM
    HV = MLA_HEADS * V_HEAD_DIM
    row = lambda w: pl.BlockSpec((tm, w), lambda i: (i, 0))
    return pl.pallas_call(
        _mla_proj_body,
        grid=(S // tm,),
        in_specs=[
            row(D), row(1), _resident(g.shape), _resident(win.shape), _resident(qlg.shape),
            _resident(wuq.shape), _resident(kvlg.shape), _resident(wukv.shape),
            _resident(qgain.shape), _resident(kgain.shape), _resident(rope_tab.shape),
        ],
        out_specs=[row(HQ), row(HQ), row(HV)],
        out_shape=[
            jax.ShapeDtypeStruct((S, HQ), BF16),
            jax.ShapeDtypeStruct((S, HQ), BF16),
            jax.ShapeDtypeStruct((S, HV), BF16),
        ],
        compiler_params=_params("parallel"),
        name="mla_proj",
    )(x, pos, g, win, qlg, wuq, kvlg, wukv, qgain, kgain, rope_tab)


def _attn_exact_path(qi, qt, k_ref, vt_ref, m_ref, l_ref, acc_ref, s_bufs, *, bq, bk):
    m_ref[...] = jnp.full(m_ref.shape, MASK_VALUE, F32)

    def scores(kb, s_ref, lo):
        kblk = k_ref[pl.ds(pl.multiple_of(kb * bk, bk), bk), :]
        s_ref[:, lo:] = jnp.dot(kblk, qt[:, lo:], preferred_element_type=F32)

    def update(kb, s_ref, lo, diagonal):
        s = s_ref[:, lo:]
        if diagonal:
            key = lax.broadcasted_iota(jnp.int32, s.shape, 0)
            qry = lax.broadcasted_iota(jnp.int32, s.shape, 1)
            s = jnp.where(key <= qry, s, MASK_VALUE)
        m_old = m_ref[:, lo:]
        m_new = jnp.maximum(m_old, jnp.max(s, axis=0, keepdims=True))
        p = jnp.exp2(s - m_new)
        alpha = jnp.exp2(m_old - m_new)
        l_ref[:, lo:] = alpha * l_ref[:, lo:] + jnp.sum(p, axis=0, keepdims=True)
        pv = jnp.dot(vt_ref[kb], p.astype(BF16), preferred_element_type=F32)
        acc_ref[:, lo:] = alpha * acc_ref[:, lo:] + pv
        m_ref[:, lo:] = m_new

    per_q = bq // bk
    n_full = qi * per_q
    scores(0, s_bufs[0], 0)

    def full_blocks(t, carry):
        base = t * per_q
        for d in range(per_q):
            scores(base + d + 1, s_bufs[(d + 1) % 2], 0)
            update(base + d, s_bufs[d % 2], 0, False)
        return carry

    lax.fori_loop(0, qi, full_blocks, 0)
    for d in range(per_q):
        if d + 1 < per_q:
            scores(n_full + d + 1, s_bufs[(d + 1) % 2], (d + 1) * bk)
        update(n_full + d, s_bufs[d % 2], d * bk, True)


def _attn_bounded_path(qi, qt, bound, k_ref, vt_ref, l_ref, acc_ref, *, bq, bk):
    per_q = bq // bk

    def block(kb, lo, diagonal):
        kblk = k_ref[pl.ds(pl.multiple_of(kb * bk, bk), bk), :]
        s = jnp.dot(kblk, qt[:, lo:], preferred_element_type=F32)
        p = jnp.exp2(s - bound[:, lo:])
        if diagonal:
            key = lax.broadcasted_iota(jnp.int32, s.shape, 0)
            qry = lax.broadcasted_iota(jnp.int32, s.shape, 1)
            p = jnp.where(key <= qry, p, 0.0)
        l_ref[:, lo:] += jnp.sum(p, axis=0, keepdims=True)
        acc_ref[:, lo:] += jnp.dot(vt_ref[kb], p.astype(BF16), preferred_element_type=F32)

    def full_blocks(t, carry):
        for d in range(per_q):
            block(t * per_q + d, 0, False)
        return carry

    lax.fori_loop(0, qi, full_blocks, 0)
    for d in range(per_q):
        block(qi * per_q + d, d * bk, True)


def _attn_body(qt_ref, k_ref, vt_ref, o_ref, m_ref, l_ref, acc_ref, s0_ref, s1_ref, kmax_ref, *, bq, bk):
    qi = pl.program_id(1)

    @pl.when(qi == 0)
    def _():
        def chunk(c, best):
            kc = k_ref[pl.ds(pl.multiple_of(c * bk, bk), bk), :].astype(F32)
            return jnp.maximum(best, jnp.max(jnp.sum(kc * kc, axis=1, keepdims=True), axis=0, keepdims=True))

        best = lax.fori_loop(0, k_ref.shape[0] // bk, chunk, jnp.zeros((1, 1), F32))
        kmax_ref[0] = best[0, 0]

    qt = qt_ref[...]
    l_ref[...] = jnp.zeros(l_ref.shape, F32)
    acc_ref[...] = jnp.zeros(acc_ref.shape, F32)
    qf = qt.astype(F32)
    q_norm2 = jnp.sum(qf * qf, axis=0, keepdims=True)
    bound = jnp.sqrt(q_norm2 * kmax_ref[0]) * ATTN_BOUND_SLACK + ATTN_BOUND_SLACK
    bounded = jnp.max(bound) <= ATTN_BOUND_LIMIT

    @pl.when(bounded)
    def _():
        _attn_bounded_path(qi, qt, bound, k_ref, vt_ref, l_ref, acc_ref, bq=bq, bk=bk)

    @pl.when(jnp.logical_not(bounded))
    def _():
        _attn_exact_path(qi, qt, k_ref, vt_ref, m_ref, l_ref, acc_ref, (s0_ref, s1_ref), bq=bq, bk=bk)

    o_ref[...] = (acc_ref[...] / l_ref[...]).T.astype(o_ref.dtype)


def _attention(qt, k, vt, bq, bk):
    S = k.shape[0]
    H = MLA_HEADS
    nkb = S // bk
    assert bq % (2 * bk) == 0 and S % bq == 0
    return pl.pallas_call(
        functools.partial(_attn_body, bq=bq, bk=bk),
        grid=(H, S // bq),
        in_specs=[
            pl.BlockSpec((QK_PAD_DIM, bq), lambda h, i: (h, i)),
            pl.BlockSpec((S, QK_PAD_DIM), lambda h, i: (0, h)),
            pl.BlockSpec((None, nkb, V_HEAD_DIM, bk), lambda h, i: (h, 0, 0, 0)),
        ],
        out_specs=pl.BlockSpec((bq, V_HEAD_DIM), lambda h, i: (i, h)),
        out_shape=jax.ShapeDtypeStruct((S, H * V_HEAD_DIM), BF16),
        scratch_shapes=[
            pltpu.VMEM((1, bq), F32),
            pltpu.VMEM((1, bq), F32),
            pltpu.VMEM((V_HEAD_DIM, bq), F32),
            pltpu.VMEM((bk, bq), F32),
            pltpu.VMEM((bk, bq), F32),
            pltpu.SMEM((1,), F32),
        ],
        compiler_params=_params("arbitrary", "arbitrary"),
        name="attention",
    )(qt, k, vt)


def _silu(x):
    return x * jax.nn.sigmoid(x)


SSM_SUB_TILE = 2 * LANES


def _ssm_z_body(x_ref, g_ref, w_ref, wdt_ref, dtb_ref, z_ref, dt_ref, h_ref):
    @pl.when(pl.program_id(1) == 0)
    def _():
        h = _rms_rows(x_ref[...], g_ref[...]).astype(BF16)
        h_ref[...] = h
        dt_raw = jnp.dot(h, wdt_ref[...], preferred_element_type=F32)
        dt_ref[...] = jax.nn.softplus(dt_raw + dtb_ref[...])

    h = h_ref[...]
    for c0 in range(0, z_ref.shape[1], SSM_SUB_TILE):
        cs = slice(c0, c0 + SSM_SUB_TILE)
        z_ref[:, cs] = _silu(jnp.dot(h, w_ref[:, cs], preferred_element_type=F32))


def _ssm_xbc_body(x_ref, g_ref, w_ref, cw_ref, cb_ref, o_ref, h_ref, carry_ref):
    i = pl.program_id(0)
    n = pl.program_id(1)
    tm = o_ref.shape[0]

    @pl.when(n == 0)
    def _():
        h_ref[...] = _rms_rows(x_ref[...], g_ref[...]).astype(BF16)

    @pl.when(i == 0)
    def _():
        carry_ref[n] = jnp.zeros(carry_ref.shape[1:], F32)

    h = h_ref[...]
    for c0 in range(0, o_ref.shape[1], SSM_SUB_TILE):
        cs = slice(c0, c0 + SSM_SUB_TILE)
        pre = jnp.dot(h, w_ref[:, cs], preferred_element_type=F32)
        full = jnp.concatenate([carry_ref[n, :, cs], pre], axis=0)
        carry_ref[n, :, cs] = pre[tm - SUBLANES:, :]
        cw = cw_ref[:, cs]
        acc = cb_ref[:, cs] + cw[SSM_CONV_WIDTH - 1:SSM_CONV_WIDTH, :] * pre
        for kk in range(SSM_CONV_WIDTH - 1):
            shift = SSM_CONV_WIDTH - 1 - kk
            acc = acc + cw[kk:kk + 1, :] * pltpu.roll(full, shift, 0)[SUBLANES:, :]
        o_ref[:, cs] = _silu(acc)


def _ssm_in(x, g, w, wdt, dtb, cw, cb, d_inner, tm, tn):
    S, D = x.shape
    N = w.shape[1]
    nz = d_inner // tn
    nx = (N - d_inner) // tn
    x_spec = pl.BlockSpec((tm, D), lambda i, n: (i, 0))
    g_spec = pl.BlockSpec((1, D), lambda i, n: (0, 0))
    zs, dt = pl.pallas_call(
        _ssm_z_body,
        grid=(S // tm, nz),
        in_specs=[
            x_spec, g_spec,
            pl.BlockSpec((D, tn), lambda i, n: (0, n)),
            pl.BlockSpec((D, LANES), lambda i, n: (0, 0)),
            pl.BlockSpec((1, LANES), lambda i, n: (0, 0)),
        ],
        out_specs=[
            pl.BlockSpec((tm, tn), lambda i, n: (i, n)),
            pl.BlockSpec((tm, LANES), lambda i, n: (i, 0)),
        ],
        out_shape=[
            jax.ShapeDtypeStruct((S, d_inner), F32),
            jax.ShapeDtypeStruct((S, LANES), F32),
        ],
        scratch_shapes=[pltpu.VMEM((tm, D), BF16)],
        compiler_params=_params("parallel", "arbitrary"),
        name="ssm_in_z",
    )(x, g, w, wdt, dtb)
    xbc = pl.pallas_call(
        _ssm_xbc_body,
        grid=(S // tm, nx),
        in_specs=[
            x_spec, g_spec,
            pl.BlockSpec((D, tn), lambda i, n: (0, nz + n)),
            pl.BlockSpec((SSM_CONV_WIDTH, tn), lambda i, n: (0, n)),
            pl.BlockSpec((1, tn), lambda i, n: (0, n)),
        ],
        out_specs=pl.BlockSpec((tm, tn), lambda i, n: (i, n)),
        out_shape=jax.ShapeDtypeStruct((S, N - d_inner), F32),
        scratch_shapes=[
            pltpu.VMEM((tm, D), BF16),
            pltpu.VMEM((nx, SUBLANES, tn), F32),
        ],
        compiler_params=_params("arbitrary", "arbitrary"),
        name="ssm_in_xbc",
    )(x, g, w, cw, cb)
    return zs, xbc, dt


def _split3(a):
    hi = a.astype(BF16)
    r1 = a - hi.astype(F32)
    mid = r1.astype(BF16)
    lo = (r1 - mid.astype(F32)).astype(BF16)
    return hi, mid, lo


def _ssd_group(gi, xs_ref, b_ref, c_ref, z_ref, dt, dt_t, a_col, a_row, d_ref, ng_ref,
               tl_ref, tu_ref, y_ref, st_ref, *, L, hpg):
    xs = xs_ref[...]
    bmat = b_ref[...]
    cmat = c_ref[...].astype(BF16)
    la = dt * a_col
    la_t = dt_t * a_row[:, 0:1]
    cum = sum(jnp.dot(tl_ref[...], p, preferred_element_type=F32) for p in _split3(la))
    cum_t = sum(jnp.dot(p, tu_ref[...], preferred_element_type=F32) for p in _split3(la_t))

    cb = lax.dot_general(cmat, bmat.astype(BF16), (((1,), (1,)), ((), ())),
                         preferred_element_type=F32)
    half = LANES
    t_out = lax.broadcasted_iota(jnp.int32, (half, half), 0)
    t_in = lax.broadcasted_iota(jnp.int32, (half, half), 1)
    causal = t_in <= t_out
    first_head = lax.broadcasted_iota(jnp.int32, (L, LANES), 1) < SSM_HEAD_DIM
    zero_q = jnp.zeros((half, half), F32)

    state = st_ref[gi]
    y_off = jnp.dot(cmat, state.astype(BF16), preferred_element_type=F32)
    cum_last = cum[L - 1:L, :]

    def col(a, j, rows):
        return jnp.broadcast_to(a[:, j:j + 1], (rows, LANES))

    ys, xws, cds = [], [], []
    for pr in range(hpg // 2):
        j0, j1 = 2 * pr, 2 * pr + 1
        sl = slice(pr * LANES, (pr + 1) * LANES)
        cum_b = (col(cum, j0, L), col(cum, j1, L))
        cum_p = jnp.where(first_head, cum_b[0], cum_b[1])
        dt_p = jnp.where(first_head, col(dt, j0, L), col(dt, j1, L))
        last_p = jnp.where(first_head[:1], col(cum_last, j0, 1), col(cum_last, j1, 1))
        xdt = xs[:, sl] * dt_p
        ms = []
        for cb_j, j in zip(cum_b, (j0, j1)):
            row = cum_t[j:j + 1, :]
            d_tl = jnp.exp2(jnp.where(causal, cb_j[:half] - row[:, :half], -jnp.inf))
            d_bl = jnp.exp2(cb_j[half:] - row[:, :half])
            d_br = jnp.exp2(jnp.where(causal, cb_j[half:] - row[:, half:], -jnp.inf))
            top = jnp.concatenate([cb[:half, :half] * d_tl, zero_q], axis=1)
            bot = jnp.concatenate([cb[half:, :half] * d_bl, cb[half:, half:] * d_br], axis=1)
            ms.append(jnp.concatenate([top, bot], axis=0).astype(BF16))
        lhs = jnp.concatenate(ms, axis=1)
        rhs = jnp.concatenate([jnp.where(first_head, xdt, 0.0).astype(BF16),
                               jnp.where(first_head, 0.0, xdt).astype(BF16)], axis=0)
        y_diag = jnp.dot(lhs, rhs, preferred_element_type=F32)
        ys.append(y_diag + y_off[:, sl] * jnp.exp2(cum_p))
        xws.append((xdt * jnp.exp2(last_p - cum_p)).astype(BF16))
        cds.append(jnp.exp2(last_p))

    xw = jnp.concatenate(xws, axis=1)
    chunk_decay = jnp.concatenate(cds, axis=1)
    st_ref[gi] = state * chunk_decay + jnp.dot(bmat.T.astype(BF16), xw, preferred_element_type=F32)

    y = jnp.concatenate(ys, axis=1) + d_ref[...] * xs
    y = y * z_ref[...]
    y = y * lax.rsqrt(jnp.mean(y * y, axis=-1, keepdims=True) + RMS_EPS)
    y_ref[...] = (y * ng_ref[...]).astype(BF16)


def _ssd_body(xs_ref, b_ref, c_ref, z_ref, dtc_ref, dtr_ref, ac_ref, ar_ref, d_ref, ng_ref,
              tl_ref, tu_ref, y_ref, st_ref, *, L, hpg, gpb):
    c = pl.program_id(0)
    gblk = pl.program_id(1)
    gw = hpg * SSM_HEAD_DIM

    @pl.when(c == 0)
    def _():
        for s in range(gpb):
            st_ref[gblk * gpb + s] = jnp.zeros(st_ref.shape[1:], F32)

    for s in range(gpb):
        cw = slice(s * gw, (s + 1) * gw)
        cn = slice(s * SSM_STATE, (s + 1) * SSM_STATE)
        _ssd_group(gblk * gpb + s, xs_ref.at[:, cw], b_ref.at[:, cn], c_ref.at[:, cn], z_ref.at[:, cw],
                   dtc_ref[s], dtr_ref[s], ac_ref[s], ar_ref[s], d_ref.at[:, cw], ng_ref.at[:, cw],
                   tl_ref, tu_ref, y_ref.at[:, cw], st_ref, L=L, hpg=hpg)


def _ssd(xbc, zs, dtc, dtr, a_col, a_row, d_tab, ng, tri_l, tri_u, d_inner, L, gpb):
    S = xbc.shape[0]
    G = SSM_GROUPS
    gw = d_inner // G
    hpg = gw // SSM_HEAD_DIM
    bw = gpb * gw
    bn = gpb * SSM_STATE
    b_blk0 = d_inner // bn
    c_blk0 = b_blk0 + G // gpb
    assert L == 2 * LANES and hpg % 2 == 0 and G % gpb == 0
    return pl.pallas_call(
        functools.partial(_ssd_body, L=L, hpg=hpg, gpb=gpb),
        grid=(S // L, G // gpb),
        in_specs=[
            pl.BlockSpec((L, bw), lambda c, g: (c, g)),
            pl.BlockSpec((L, bn), lambda c, g: (c, b_blk0 + g)),
            pl.BlockSpec((L, bn), lambda c, g: (c, c_blk0 + g)),
            pl.BlockSpec((L, bw), lambda c, g: (c, g)),
            pl.BlockSpec((gpb, L, LANES), lambda c, g: (g, c, 0)),
            pl.BlockSpec((gpb, hpg, L), lambda c, g: (g, 0, c)),
            pl.BlockSpec((gpb, 1, LANES), lambda c, g: (g, 0, 0)),
            pl.BlockSpec((gpb, hpg, LANES), lambda c, g: (g, 0, 0)),
            pl.BlockSpec((1, bw), lambda c, g: (0, g)),
            pl.BlockSpec((1, bw), lambda c, g: (0, g)),
            pl.BlockSpec((L, L), lambda c, g: (0, 0)),
            pl.BlockSpec((L, L), lambda c, g: (0, 0)),
        ],
        out_specs=pl.BlockSpec((L, bw), lambda c, g: (c, g)),
        out_shape=jax.ShapeDtypeStruct((S, d_inner), BF16),
        scratch_shapes=[pltpu.VMEM((G, SSM_STATE, gw), F32)],
        compiler_params=_params("arbitrary", "arbitrary"),
        name="ssd",
    )(xbc, xbc, xbc, zs, dtc, dtr, a_col, a_row, d_tab, ng, tri_l, tri_u)


def _rope_layout_index():
    x1 = jnp.arange(HALF_ROPE)
    x2 = x1 + HALF_ROPE
    return jnp.concatenate([x1, x2, x2, x1])


def _mla_layer(x, pos, norm_g, w_in, q_lora_g, w_uq, kv_lora_g, w_ukv, qn_g, kn_g, w_o, t):
    S, D = x.shape
    rl = _rope_layout_index()
    lora = Q_LORA_RANK + KV_LORA_RANK
    win_p = jnp.concatenate([w_in[:, :lora], w_in[:, lora + rl]], axis=1).astype(BF16)
    head_cols = jnp.concatenate([jnp.arange(QK_NOPE_DIM), QK_NOPE_DIM + rl])
    uq_cols = (jnp.arange(MLA_HEADS)[:, None] * QK_HEAD_DIM + head_cols[None, :]).reshape(-1)
    wuq_p = w_uq[:, uq_cols].astype(BF16)
    qgain = (qn_g[head_cols] * (QK_HEAD_DIM ** -0.5 * LOG2_E))[None, :]
    kgain = kn_g[head_cols][None, :]
    inv_freq = ROPE_THETA ** (-jnp.arange(0, QK_ROPE_DIM, 2, dtype=F32) / QK_ROPE_DIM)
    ones = jnp.ones((HALF_ROPE,), F32)
    zeros = jnp.zeros((2 * HALF_ROPE,), F32)
    rope_tab = jnp.zeros((SUBLANES, LANES), F32)
    rope_tab = rope_tab.at[0].set(jnp.tile(inv_freq, 4))
    rope_tab = rope_tab.at[1].set(jnp.concatenate([ones, ones, zeros]))
    rope_tab = rope_tab.at[2].set(jnp.concatenate([-ones, ones, zeros]))

    q, k, v = _mla_proj(x, pos, norm_g[None, :], win_p, q_lora_g[None, :], wuq_p,
                        kv_lora_g[None, :], w_ukv.astype(BF16), qgain, kgain, rope_tab, t["proj_tm"])
    bk = t["attn_bk"]
    vt = v.reshape(S // bk, bk, MLA_HEADS, V_HEAD_DIM).transpose(2, 0, 3, 1)
    o = _attention(q.T, k, vt, t["attn_bq"], bk)
    return _residual_matmul(x, o, w_o.astype(BF16), t["out_tm"])


def _ssm_layer(x, norm_g, w_in, conv_w, conv_b, dt_bias, a_log, d_skip, out_norm_g, w_out, t):
    S, D = x.shape
    n_heads = dt_bias.shape[0]
    d_inner = n_heads * SSM_HEAD_DIM
    G = SSM_GROUPS
    hpg = n_heads // G
    n_zx = w_in.shape[1] - n_heads
    L = SSM_CHUNK if S % SSM_CHUNK == 0 else S
    pad = LANES - n_heads
    wdt = jnp.pad(w_in[:, n_zx:], ((0, 0), (0, pad))).astype(BF16)
    dtb = jnp.pad(dt_bias, (0, pad))[None, :]
    zs, xbc, dt = _ssm_in(x, norm_g[None, :], w_in[:, :n_zx].astype(BF16), wdt, dtb,
                          conv_w, conv_b[None, :], d_inner, t["ssm_tm"], t["ssm_tn"])
    dt_g = dt[:, :n_heads].reshape(S, G, hpg).transpose(1, 0, 2)
    dtc = jnp.pad(dt_g, ((0, 0), (0, 0), (0, LANES - hpg)))
    dtr = dt_g.transpose(0, 2, 1)
    a = (-jnp.exp(a_log) * LOG2_E).reshape(G, hpg)
    a_col = jnp.pad(a, ((0, 0), (0, LANES - hpg)))[:, None, :]
    a_row = jnp.broadcast_to(a[:, :, None], (G, hpg, LANES))
    d_tab = jnp.repeat(d_skip, SSM_HEAD_DIM)[None, :]
    tri_l = jnp.tril(jnp.ones((L, L), BF16))
    y = _ssd(xbc, zs, dtc, dtr, a_col, a_row, d_tab, out_norm_g[None, :], tri_l, tri_l.T, d_inner, L,
             t["ssd_groups"])
    return _residual_matmul(x, y, w_out.astype(BF16), t["out_tm"])


def kernel(x, positions, mix_norm_g, mlp_norm_g, mlp_w_in, mlp_w_out, mla_w_in, mla_q_norm_g, mla_w_uq, mla_kv_norm_g, mla_w_ukv, mla_qk_norm_q, mla_qk_norm_k, mla_w_o, ssm_w_in, ssm_conv_w, ssm_conv_b, ssm_dt_bias, ssm_a_log, ssm_d, ssm_norm_g, ssm_w_out):
    B, S, D = x.shape
    depth = mix_norm_g.shape[0]
    t = _tiles(S)
    outs = []
    for b in range(B):
        xb = x[b]
        pos = positions[b][:, None]
        for i in range(depth):
            j = i // 2
            if i % 2 == 0:
                xb = _mla_layer(xb, pos, mix_norm_g[i], mla_w_in[j], mla_q_norm_g[j], mla_w_uq[j],
                                mla_kv_norm_g[j], mla_w_ukv[j], mla_qk_norm_q[j], mla_qk_norm_k[j],
                                mla_w_o[j], t)
            else:
                xb = _ssm_layer(xb, mix_norm_g[i], ssm_w_in[j], ssm_conv_w[j], ssm_conv_b[j],
                                ssm_dt_bias[j], ssm_a_log[j], ssm_d[j], ssm_norm_g[j], ssm_w_out[j], t)
            xb = _mlp(xb, mlp_norm_g[i][None, :], mlp_w_in[i].astype(BF16), mlp_w_out[i].astype(BF16),
                      t["mlp_tm"], t["mlp_tf"])
        outs.append(xb)
    return jnp.stack(outs)
```
